```python
import math
import jax, jax.numpy as jnp
from jax import lax
import numpy as np

D_MODEL = 1024
BATCH = 4
SEQ = 4096
DEPTH = 4
DEC_BATCH = 32
DEC_SEQ = 4
PAST_LEN = 8192
PAGE_SIZE = 128

N_MIXERS = 2
N_LAYERS_A = (DEPTH + 1) // 2
N_LAYERS_B = DEPTH // 2
HEAD_DIM_A = 64
N_HEADS_A = D_MODEL // (2 * HEAD_DIM_A)
A_QK_WIDTH = 2 * N_HEADS_A * HEAD_DIM_A
A_V_WIDTH = N_HEADS_A * 2 * HEAD_DIM_A
HEAD_DIM_B = 64
N_HEADS_B = D_MODEL // HEAD_DIM_B
B_W = N_HEADS_B * HEAD_DIM_B
N_HEADS_IDX = 8
HEAD_DIM_IDX = 64
TOPK_MAX = 256
B_SPLITS = [B_W, 2 * B_W, 3 * B_W, 3 * B_W + N_HEADS_IDX * HEAD_DIM_IDX,
            3 * B_W + N_HEADS_IDX * HEAD_DIM_IDX + HEAD_DIM_IDX]
B_QKV_WIDTH = B_SPLITS[-1] + N_HEADS_IDX
D_FF = -(-8 * D_MODEL // (3 * 256)) * 256
D_PLE = 256
ROPE_THETA = 10000.0
RMS_EPS = 1e-6
Q_BLOCK = 128

kernel_name = 'hybrid_diffattn_dsa_decode_step'

F32 = jnp.float32


def rms_norm(x, g):
    xf = x.astype(F32)
    y = xf * lax.rsqrt(jnp.mean(xf * xf, axis=-1, keepdims=True) + RMS_EPS)
    return (y * g.astype(F32)).astype(x.dtype)


def rope(x, pos):
    d = x.shape[-1]
    half = d // 2
    inv_freq = ROPE_THETA ** (-jnp.arange(half, dtype=F32) * 2.0 / d)
    ang = pos.astype(F32)[:, None] * inv_freq[None, :]
    cos = jnp.cos(ang)[:, None, :]
    sin = jnp.sin(ang)[:, None, :]
    xf = x.astype(F32)
    x1, x2 = xf[..., :half], xf[..., half:]
    return jnp.concatenate([x1 * cos - x2 * sin, x2 * cos + x1 * sin], axis=-1).astype(x.dtype)


def to_blocks(a):
    b, t = a.shape[:2]
    return a.reshape((b, t // Q_BLOCK, Q_BLOCK) + a.shape[2:]).swapaxes(0, 1)


def from_blocks(a):
    nb, b, qb = a.shape[:3]
    return a.swapaxes(0, 1).reshape((b, nb * qb) + a.shape[3:])


def gather_rows(rows, ids):
    return jax.vmap(lambda r, i: r[i])(rows, ids)


def diff_lambda(lam_p, lam_init):
    lp = lam_p.astype(F32)
    return jnp.exp(jnp.sum(lp[0] * lp[1])) - jnp.exp(jnp.sum(lp[2] * lp[3])) + lam_init


def diff_project(h, w_qkv, q_gain, k_gain, pos):
    b, t, _ = h.shape
    q, k, v = jnp.split(h @ w_qkv, [A_QK_WIDTH, 2 * A_QK_WIDTH], axis=-1)
    q = rope(rms_norm(q.reshape(b, t, 2 * N_HEADS_A, HEAD_DIM_A), q_gain), pos)
    k = rope(rms_norm(k.reshape(b, t, 2 * N_HEADS_A, HEAD_DIM_A), k_gain), pos)
    v = v.reshape(b, t, N_HEADS_A, 2 * HEAD_DIM_A)
    return q, k, v


def diff_core(q, k, v, q_pos, k_pos, lam):
    b, tq = q.shape[:2]
    s = jnp.einsum('bqhd,bkhd->bhqk', q.astype(F32), k.astype(F32)) * HEAD_DIM_A ** -0.5
    s = jnp.where((k_pos[None, :] <= q_pos[:, None])[None, None], s, -jnp.inf)
    p = jax.nn.softmax(s, axis=-1).reshape(b, N_HEADS_A, 2, tq, -1)
    a = p[:, :, 0] - lam * p[:, :, 1]
    return jnp.einsum('bhqk,bkhe->bqhe', a, v.astype(F32))


def diff_output(o, lam_init, subln_gain, w_o, dtype):
    b, t = o.shape[:2]
    o = rms_norm(o, subln_gain) * (1.0 - lam_init)
    return o.reshape(b, t, A_V_WIDTH).astype(dtype) @ w_o


def diff_attn_prompt(h, w_qkv, w_o, q_gain, k_gain, lam_p, subln_gain, lam_init):
    t = h.shape[1]
    pos = jnp.arange(t, dtype=jnp.int32)
    q, k, v = diff_project(h, w_qkv, q_gain, k_gain, pos)
    lam = diff_lambda(lam_p, lam_init)
    o = lax.map(lambda a: diff_core(a[0], k, v, a[1], pos, lam),
                (to_blocks(q), pos.reshape(-1, Q_BLOCK)))
    return diff_output(from_blocks(o), lam_init, subln_gain, w_o, h.dtype), k, v


def diff_attn_sample(h, cache_k, cache_v, page_table, w_qkv, w_o, q_gain, k_gain, lam_p, subln_gain, lam_init):
    bd, t, _ = h.shape
    past = page_table.shape[1] * cache_k.shape[1]
    pos = past + jnp.arange(t, dtype=jnp.int32)
    q, k, v = diff_project(h, w_qkv, q_gain, k_gain, pos)
    lam = diff_lambda(lam_p, lam_init)
    k_past = cache_k[page_table].reshape(bd, past, 2 * N_HEADS_A, HEAD_DIM_A)
    v_past = cache_v[page_table].reshape(bd, past, N_HEADS_A, 2 * HEAD_DIM_A)
    k_all = jnp.concatenate([k_past.astype(k.dtype), k], axis=1)
    v_all = jnp.concatenate([v_past.astype(v.dtype), v], axis=1)
    k_pos = jnp.arange(past + t, dtype=jnp.int32)
    o = diff_core(q, k_all, v_all, pos, k_pos, lam)
    return diff_output(o, lam_init, subln_gain, w_o, h.dtype), k, v


def dsa_project(h, w_qkv, q_gain, k_gain, kidx_gain, pos):
    b, t, _ = h.shape
    q, k, v, qi, ki, wi = jnp.split(h @ w_qkv, B_SPLITS, axis=-1)
    q = rope(rms_norm(q.reshape(b, t, N_HEADS_B, HEAD_DIM_B), q_gain), pos)
    k = rope(rms_norm(k.reshape(b, t, N_HEADS_B, HEAD_DIM_B), k_gain), pos)
    v = v.reshape(b, t, N_HEADS_B, HEAD_DIM_B)
    qi = rope(qi.reshape(b, t, N_HEADS_IDX, HEAD_DIM_IDX), pos)
    ki = rope(rms_norm(ki, kidx_gain)[:, :, None, :], pos)[:, :, 0, :]
    wi = wi * N_HEADS_IDX ** -0.5
    return q, k, v, qi, ki, wi


def indexer_topk(qi, wi, ki, q_pos, k_pos, k_sel):
    s = jnp.einsum('bqhd,bkd->bqhk', qi.astype(F32), ki.astype(F32)) * HEAD_DIM_IDX ** -0.5
    score = jnp.einsum('bqh,bqhk->bqk', wi.astype(F32), jax.nn.relu(s))
    score = jnp.where((k_pos[None, :] <= q_pos[:, None])[None], score, -jnp.inf)
    _, idx = lax.top_k(score, k_sel)
    valid = idx <= q_pos[None, :, None]
    return idx, valid


def sparse_core(q, k_g, v_g, valid):
    s = jnp.einsum('bqhd,bqkhd->bqhk', q.astype(F32), k_g.astype(F32)) * HEAD_DIM_B ** -0.5
    s = jnp.where(valid[:, :, None, :], s, -jnp.inf)
    p = jax.nn.softmax(s, axis=-1)
    return jnp.einsum('bqhk,bqkhd->bqhd', p, v_g.astype(F32))


def dsa_output(o, w_o, dtype):
    b, t = o.shape[:2]
    return o.reshape(b, t, B_W).astype(dtype) @ w_o


def dsa_prompt(h, w_qkv, w_o, q_gain, k_gain, kidx_gain):
    t = h.shape[1]
    pos = jnp.arange(t, dtype=jnp.int32)
    q, k, v, qi, ki, wi = dsa_project(h, w_qkv, q_gain, k_gain, kidx_gain, pos)
    k_sel = min(TOPK_MAX, t // 4)

    def block(a):
        qb, qib, wib, pb = a
        idx, valid = indexer_topk(qib, wib, ki, pb, pos, k_sel)
        return sparse_core(qb, gather_rows(k, idx), gather_rows(v, idx), valid)

    o = lax.map(block, (to_blocks(q), to_blocks(qi), to_blocks(wi), pos.reshape(-1, Q_BLOCK)))
    return dsa_output(from_blocks(o), w_o, h.dtype), k, v, ki


def dsa_sample(h, cache_k, cache_v, cache_kidx, page_table, w_qkv, w_o, q_gain, k_gain, kidx_gain):
    bd, t, _ = h.shape
    ps = cache_k.shape[1]
    past = page_table.shape[1] * ps
    pos = past + jnp.arange(t, dtype=jnp.int32)
    q, k, v, qi, ki, wi = dsa_project(h, w_qkv, q_gain, k_gain, kidx_gain, pos)
    ki_past = cache_kidx[page_table].reshape(bd, past, HEAD_DIM_IDX)
    ki_all = jnp.concatenate([ki_past.astype(ki.dtype), ki], axis=1)
    k_pos = jnp.arange(past + t, dtype=jnp.int32)
    k_sel = min(TOPK_MAX, (past + t) // 4)
    idx, valid = indexer_topk(qi, wi, ki_all, pos, k_pos, k_sel)
    in_past = (idx < past)[..., None, None]
    pidx = jnp.minimum(idx, past - 1)
    phys = gather_rows(page_table, pidx // ps)
    off = pidx % ps
    nidx = jnp.clip(idx - past, 0, t - 1)
    k_g = jnp.where(in_past, cache_k[phys, off].astype(k.dtype), gather_rows(k, nidx))
    v_g = jnp.where(in_past, cache_v[phys, off].astype(v.dtype), gather_rows(v, nidx))
    o = sparse_core(q, k_g, v_g, valid)
    return dsa_output(o, w_o, h.dtype), k, v, ki


def ffn_and_ple(x, p_i, norm_ffn_i, w_ffn_in_i, w_ffn_out_i, norm_ple_i, w_ple_gate_i, w_ple_proj_i):
    h = rms_norm(x, norm_ffn_i)
    g, u = jnp.split(h @ w_ffn_in_i, 2, axis=-1)
    x = x + (jax.nn.silu(g) * u) @ w_ffn_out_i
    gate = jax.nn.sigmoid((rms_norm(x, norm_ple_i) @ w_ple_gate_i).astype(F32)).astype(x.dtype)
    return x + gate * (p_i @ w_ple_proj_i)


def setup_inputs(seed: int = 0) -> dict:
    key = jax.random.key(seed)
    ks = jax.random.split(key, 32)
    n_pages = PAST_LEN // PAGE_SIZE
    n_pool = (DEC_BATCH * n_pages * 5) // 4

    def normal(k, shape, scale=1.0):
        return jax.random.normal(k, shape, F32) * scale

    def gain(k, shape):
        return 1.0 + 0.05 * jax.random.normal(k, shape, F32)

    page_table = jax.random.permutation(ks[7], n_pool)[: DEC_BATCH * n_pages]
    page_table = page_table.reshape(DEC_BATCH, n_pages).astype(jnp.int32)
    return {
        'x_prompt': normal(ks[0], (BATCH, SEQ, D_MODEL)),
        'x_sample': normal(ks[1], (DEC_BATCH, DEC_SEQ, D_MODEL)),
        'cache_a_k': normal(ks[2], (N_LAYERS_A, n_pool, PAGE_SIZE, 2 * N_HEADS_A, HEAD_DIM_A)),
        'cache_a_v': normal(ks[3], (N_LAYERS_A, n_pool, PAGE_SIZE, N_HEADS_A, 2 * HEAD_DIM_A)),
        'cache_b_k': normal(ks[4], (N_LAYERS_B, n_pool, PAGE_SIZE, N_HEADS_B, HEAD_DIM_B)),
        'cache_b_v': normal(ks[5], (N_LAYERS_B, n_pool, PAGE_SIZE, N_HEADS_B, HEAD_DIM_B)),
        'cache_b_kidx': normal(ks[6], (N_LAYERS_B, n_pool, PAGE_SIZE, HEAD_DIM_IDX)),
        'page_table': page_table,
        'p_prompt': normal(ks[8], (DEPTH, BATCH, SEQ, D_PLE)),
        'p_sample': normal(ks[9], (DEPTH, DEC_BATCH, DEC_SEQ, D_PLE)),
        'norm_mix': gain(ks[10], (DEPTH, D_MODEL)),
        'norm_ffn': gain(ks[11], (DEPTH, D_MODEL)),
        'norm_ple': gain(ks[12], (DEPTH, D_MODEL)),
        'a_w_qkv': normal(ks[13], (N_LAYERS_A, D_MODEL, 2 * A_QK_WIDTH + A_V_WIDTH), D_MODEL ** -0.5),
        'a_w_o': normal(ks[14], (N_LAYERS_A, A_V_WIDTH, D_MODEL), A_V_WIDTH ** -0.5),
        'a_q_norm': gain(ks[15], (N_LAYERS_A, HEAD_DIM_A)),
        'a_k_norm': gain(ks[16], (N_LAYERS_A, HEAD_DIM_A)),
        'a_lambda': normal(ks[17], (N_LAYERS_A, 4, HEAD_DIM_A), 0.1),
        'a_subln': gain(ks[18], (N_LAYERS_A, 2 * HEAD_DIM_A)),
        'b_w_qkv': normal(ks[19], (N_LAYERS_B, D_MODEL, B_QKV_WIDTH), D_MODEL ** -0.5),
        'b_w_o': normal(ks[20], (N_LAYERS_B, B_W, D_MODEL), B_W ** -0.5),
        'b_q_norm': gain(ks[21], (N_LAYERS_B, HEAD_DIM_B)),
        'b_k_norm': gain(ks[22], (N_LAYERS_B, HEAD_DIM_B)),
        'b_kidx_norm': gain(ks[23], (N_LAYERS_B, HEAD_DIM_IDX)),
        'w_ffn_in': normal(ks[24], (DEPTH, D_MODEL, 2 * D_FF), D_MODEL ** -0.5),
        'w_ffn_out': normal(ks[25], (DEPTH, D_FF, D_MODEL), D_FF ** -0.5),
        'w_ple_gate': normal(ks[26], (DEPTH, D_MODEL, D_MODEL), D_MODEL ** -0.5),
        'w_ple_proj': normal(ks[27], (DEPTH, D_PLE, D_MODEL), 0.5 * D_PLE ** -0.5),
    }


def reference(x_prompt, x_sample, cache_a_k, cache_a_v, cache_b_k, cache_b_v, cache_b_kidx,
              page_table, p_prompt, p_sample, norm_mix, norm_ffn, norm_ple,
              a_w_qkv, a_w_o, a_q_norm, a_k_norm, a_lambda, a_subln,
              b_w_qkv, b_w_o, b_q_norm, b_k_norm, b_kidx_norm,
              w_ffn_in, w_ffn_out, w_ple_gate, w_ple_proj):
    xp, xs = x_prompt, x_sample
    a_kp, a_vp, a_ks, a_vs = [], [], [], []
    b_kp, b_vp, b_kip, b_ks, b_vs, b_kis = [], [], [], [], [], []
    for i in range(DEPTH):
        j = i // N_MIXERS
        hp = rms_norm(xp, norm_mix[i])
        hs = rms_norm(xs, norm_mix[i])
        if i % N_MIXERS == 0:
            lam_init = 0.8 - 0.6 * math.exp(-0.3 * i)
            mp, kp, vp = diff_attn_prompt(hp, a_w_qkv[j], a_w_o[j], a_q_norm[j], a_k_norm[j],
                                          a_lambda[j], a_subln[j], lam_init)
            ms, k_s, v_s = diff_attn_sample(hs, cache_a_k[j], cache_a_v[j], page_table,
                                            a_w_qkv[j], a_w_o[j], a_q_norm[j], a_k_norm[j],
                                            a_lambda[j], a_subln[j], lam_init)
            a_kp.append(kp); a_vp.append(vp); a_ks.append(k_s); a_vs.append(v_s)
        else:
            mp, kp, vp, kip = dsa_prompt(hp, b_w_qkv[j], b_w_o[j], b_q_norm[j], b_k_norm[j], b_kidx_norm[j])
            ms, k_s, v_s, kis = dsa_sample(hs, cache_b_k[j], cache_b_v[j], cache_b_kidx[j], page_table,
                                           b_w_qkv[j], b_w_o[j], b_q_norm[j], b_k_norm[j], b_kidx_norm[j])
            b_kp.append(kp); b_vp.append(vp); b_kip.append(kip)
            b_ks.append(k_s); b_vs.append(v_s); b_kis.append(kis)
        xp = ffn_and_ple(xp + mp, p_prompt[i], norm_ffn[i], w_ffn_in[i], w_ffn_out[i],
                         norm_ple[i], w_ple_gate[i], w_ple_proj[i])
        xs = ffn_and_ple(xs + ms, p_sample[i], norm_ffn[i], w_ffn_in[i], w_ffn_out[i],
                         norm_ple[i], w_ple_gate[i], w_ple_proj[i])
    return (xp, xs,
            jnp.stack(a_kp), jnp.stack(a_vp), jnp.stack(b_kp), jnp.stack(b_vp), jnp.stack(b_kip),
            jnp.stack(a_ks), jnp.stack(a_vs), jnp.stack(b_ks), jnp.stack(b_vs), jnp.stack(b_kis))
```

```python
import functools
import math

import jax
import jax.numpy as jnp
from jax import lax
from jax.experimental import pallas as pl
from jax.experimental.pallas import tpu as pltpu

F32 = jnp.float32
BF16 = jnp.bfloat16
I32 = jnp.int32

D_MODEL = 1024
HEAD_DIM = 64
LANES = 128
N_HEADS_IDX = 8
TOPK = 256
ROPE_THETA = 10000.0
RMS_EPS = 1e-6
D_FF = 2816
D_PLE = 256
PAGE = 128
NEG = -1e30
INT_MIN = -(2 ** 31)
VMEM_LIMIT = 56 * 1024 * 1024

_NT = (((1,), (1,)), ((), ()))


def _dot(a, b):
    return jnp.dot(a, b, preferred_element_type=F32)


def _dot_nt(a, b):
    return lax.dot_general(a, b, _NT, preferred_element_type=F32)


def _cparams(sem):
    return pltpu.CompilerParams(dimension_semantics=sem, vmem_limit_bytes=VMEM_LIMIT)


def _rms(x, g):
    return x * lax.rsqrt(jnp.mean(x * x, axis=-1, keepdims=True) + RMS_EPS) * g


def _lane_iota(shape):
    return lax.broadcasted_iota(I32, shape, len(shape) - 1)


def _head_norm(y, gmat, g):
    ss = _dot((y * y).astype(BF16), gmat)
    return y * lax.rsqrt(ss * (1.0 / HEAD_DIM) + RMS_EPS) * g


def _rope(y, cos, sin, first_half):
    rot = jnp.where(first_half, pltpu.roll(y, LANES - HEAD_DIM // 2, 1), pltpu.roll(y, HEAD_DIM // 2, 1))
    return y * cos + rot * sin


def _rope_tables(pos):
    half = HEAD_DIM // 2
    inv_freq = ROPE_THETA ** (-jnp.arange(half, dtype=F32) * 2.0 / HEAD_DIM)
    ang = pos.astype(F32)[:, None] * inv_freq[None, :]
    cos, sin = jnp.cos(ang), jnp.sin(ang)
    return (jnp.concatenate([cos, cos, cos, cos], axis=-1),
            jnp.concatenate([-sin, sin, -sin, sin], axis=-1))


def _group_matrix():
    r = lax.broadcasted_iota(I32, (LANES, LANES), 0) // HEAD_DIM
    c = lax.broadcasted_iota(I32, (LANES, LANES), 1) // HEAD_DIM
    return (r == c).astype(BF16)


def _sortable(score):
    score = jnp.where(score == 0.0, 0.0, score)
    bits = lax.bitcast_convert_type(score, I32)
    return bits ^ ((bits >> 31) & 0x7FFFFFFF)


def _kth_largest(count_ge, shape):
    t0 = jnp.where(count_ge(jnp.zeros(shape, I32)) >= TOPK, 0, INT_MIN).astype(I32)

    def body(b, t):
        cand = t | (1 << (30 - b))
        return jnp.where(count_ge(cand) >= TOPK, cand, t)

    return lax.fori_loop(0, 31, body, t0)


def _tie_cutoff(count_tie_lt, need, shape, nbits):
    def body(b, j):
        cand = j | (1 << (nbits - 1 - b))
        return jnp.where(count_tie_lt(cand) <= need, cand, j)

    return lax.fori_loop(0, nbits, body, jnp.zeros(shape, I32))


def _proj_a_kernel(x_ref, gn_ref, w_ref, qg_ref, kg_ref, cos_ref, sin_ref, gmat_ref,
                   q_ref, k32_ref, k16_ref, v32_ref, v16_ref):
    h = _rms(x_ref[...], gn_ref[...]).astype(BF16)
    cos, sin, gmat = cos_ref[...], sin_ref[...], gmat_ref[...]
    first = (_lane_iota((1, LANES)) % HEAD_DIM) < HEAD_DIM // 2
    yq = _dot(h, w_ref[:, 0:D_MODEL])
    yk = _dot(h, w_ref[:, D_MODEL:2 * D_MODEL])
    for c in range(D_MODEL // LANES):
        sl = slice(c * LANES, (c + 1) * LANES)
        q = _rope(_head_norm(yq[:, sl], gmat, qg_ref[...]), cos, sin, first)
        q_ref[:, sl] = (q * HEAD_DIM ** -0.5).astype(BF16)
        k = _rope(_head_norm(yk[:, sl], gmat, kg_ref[...]), cos, sin, first)
        k32_ref[:, sl] = k
        k16_ref[:, sl] = k.astype(BF16)
    v = _dot(h, w_ref[:, 2 * D_MODEL:3 * D_MODEL])
    v32_ref[...] = v
    v16_ref[...] = v.astype(BF16)


def _proj_a(x, gn, w, qg, kg, cos, sin, tm):
    m = x.shape[0]
    nt = cos.shape[0] // tm
    row = lambda i: (i, 0)
    fixed = lambda i: (0, 0)
    wide = pl.BlockSpec((tm, D_MODEL), row)
    return pl.pallas_call(
        _proj_a_kernel,
        grid=(m // tm,),
        in_specs=[wide,
                  pl.BlockSpec((1, D_MODEL), fixed),
                  pl.BlockSpec((D_MODEL, 3 * D_MODEL), fixed),
                  pl.BlockSpec((1, LANES), fixed),
                  pl.BlockSpec((1, LANES), fixed),
                  pl.BlockSpec((tm, LANES), lambda i: (i % nt, 0)),
                  pl.BlockSpec((tm, LANES), lambda i: (i % nt, 0)),
                  pl.BlockSpec((LANES, LANES), fixed)],
        out_specs=[wide, wide, wide, wide, wide],
        out_shape=[jax.ShapeDtypeStruct((m, D_MODEL), BF16),
                   jax.ShapeDtypeStruct((m, D_MODEL), F32),
                   jax.ShapeDtypeStruct((m, D_MODEL), BF16),
                   jax.ShapeDtypeStruct((m, D_MODEL), F32),
                   jax.ShapeDtypeStruct((m, D_MODEL), BF16)],
        compiler_params=_cparams(("arbitrary",)),
        name="proj_a",
    )(x, gn, w, qg, kg, cos, sin, _group_matrix())


B_QI = 3 * D_MODEL
B_TAIL = B_QI + N_HEADS_IDX * HEAD_DIM


def _proj_b_kernel(x_ref, gn_ref, w_ref, wt_ref, qg_ref, kg_ref, kig_ref, cos_ref, sin_ref, gmat_ref,
                   q_ref, k32_ref, k16_ref, v32_ref, v16_ref, qi_ref, tail_ref, ki16_ref):
    h = _rms(x_ref[...], gn_ref[...]).astype(BF16)
    cos, sin, gmat = cos_ref[...], sin_ref[...], gmat_ref[...]
    lane = _lane_iota((1, LANES))
    first = (lane % HEAD_DIM) < HEAD_DIM // 2
    yq = _dot(h, w_ref[:, 0:D_MODEL])
    yk = _dot(h, w_ref[:, D_MODEL:2 * D_MODEL])
    for c in range(D_MODEL // LANES):
        sl = slice(c * LANES, (c + 1) * LANES)
        q = _rope(_head_norm(yq[:, sl], gmat, qg_ref[...]), cos, sin, first)
        q_ref[:, sl] = (q * HEAD_DIM ** -0.5).astype(BF16)
        k = _rope(_head_norm(yk[:, sl], gmat, kg_ref[...]), cos, sin, first)
        k32_ref[:, sl] = k
        k16_ref[:, sl] = k.astype(BF16)
    v = _dot(h, w_ref[:, 2 * D_MODEL:3 * D_MODEL])
    v32_ref[...] = v
    v16_ref[...] = v.astype(BF16)
    yi = _dot(h, w_ref[:, B_QI:B_TAIL])
    for c in range(N_HEADS_IDX * HEAD_DIM // LANES):
        qi = _rope(yi[:, c * LANES:(c + 1) * LANES], cos, sin, first) * HEAD_DIM ** -0.5
        qi_ref[2 * c] = qi[:, :HEAD_DIM].astype(BF16)
        qi_ref[2 * c + 1] = qi[:, HEAD_DIM:].astype(BF16)
    yt = _dot(h, wt_ref[...])
    is_key = lane < HEAD_DIM
    ms = jnp.sum(jnp.where(is_key, yt * yt, 0.0), axis=-1, keepdims=True) * (1.0 / HEAD_DIM)
    ki = _rope(yt * lax.rsqrt(ms + RMS_EPS) * kig_ref[...], cos, sin, first)
    tail = jnp.where(is_key, ki, yt * N_HEADS_IDX ** -0.5)
    tail_ref[...] = tail
    ki16_ref[...] = tail[:, :HEAD_DIM].astype(BF16)


def _proj_b(x, gn, w, wt, qg, kg, kig, cos, sin, tm):
    m = x.shape[0]
    nt = cos.shape[0] // tm
    row = lambda i: (i, 0)
    fixed = lambda i: (0, 0)
    wide = pl.BlockSpec((tm, D_MODEL), row)
    return pl.pallas_call(
        _proj_b_kernel,
        grid=(m // tm,),
        in_specs=[wide,
                  pl.BlockSpec((1, D_MODEL), fixed),
                  pl.BlockSpec((D_MODEL, B_TAIL), fixed),
                  pl.BlockSpec((D_MODEL, LANES), fixed),
                  pl.BlockSpec((1, LANES), fixed),
                  pl.BlockSpec((1, LANES), fixed),
                  pl.BlockSpec((1, LANES), fixed),
                  pl.BlockSpec((tm, LANES), lambda i: (i % nt, 0)),
                  pl.BlockSpec((tm, LANES), lambda i: (i % nt, 0)),
                  pl.BlockSpec((LANES, LANES), fixed)],
        out_specs=[wide, wide, wide, wide, wide,
                   pl.BlockSpec((N_HEADS_IDX, tm, HEAD_DIM), lambda i: (0, i, 0)),
                   pl.BlockSpec((tm, LANES), row),
                   pl.BlockSpec((tm, HEAD_DIM), row)],
        out_shape=[jax.ShapeDtypeStruct((m, D_MODEL), BF16),
                   jax.ShapeDtypeStruct((m, D_MODEL), F32),
                   jax.ShapeDtypeStruct((m, D_MODEL), BF16),
                   jax.ShapeDtypeStruct((m, D_MODEL), F32),
                   jax.ShapeDtypeStruct((m, D_MODEL), BF16),
                   jax.ShapeDtypeStruct((N_HEADS_IDX, m, HEAD_DIM), BF16),
                   jax.ShapeDtypeStruct((m, LANES), F32),
                   jax.ShapeDtypeStruct((m, HEAD_DIM), BF16)],
        compiler_params=_cparams(("arbitrary",)),
        name="proj_b",
    )(x, gn, w, wt, qg, kg, kig, cos, sin, _group_matrix())


FF_CHUNK = 256


def _tail_kernel(x_ref, o_ref, p_ref, wo_ref, gf_ref, win_ref, wout_ref, gp_ref, wg_ref, wp_ref,
                 y_ref, acc_ref):
    x1 = x_ref[...] + _dot(o_ref[...], wo_ref[...])
    h = _rms(x1, gf_ref[...]).astype(BF16)
    for c in range(D_FF // FF_CHUNK):
        g = _dot(h, win_ref[:, c * FF_CHUNK:(c + 1) * FF_CHUNK])
        u = _dot(h, win_ref[:, D_FF + c * FF_CHUNK:D_FF + (c + 1) * FF_CHUNK])
        a = (g * jax.nn.sigmoid(g) * u).astype(BF16)
        d = _dot(a, wout_ref[c * FF_CHUNK:(c + 1) * FF_CHUNK, :])
        if c == 0:
            acc_ref[...] = x1 + d
        else:
            acc_ref[...] += d
    x2 = acc_ref[...]
    gate = jax.nn.sigmoid(_dot(_rms(x2, gp_ref[...]).astype(BF16), wg_ref[...]))
    y_ref[...] = x2 + gate * _dot(p_ref[...].astype(BF16), wp_ref[...])


def _tail(x, o, p, wo, gf, win, wout, gp, wg, wp, tm):
    m = x.shape[0]
    row = lambda i: (i, 0)
    fixed = lambda i: (0, 0)
    once = dict(pipeline_mode=pl.Buffered(1))
    return pl.pallas_call(
        _tail_kernel,
        grid=(m // tm,),
        in_specs=[pl.BlockSpec((tm, D_MODEL), row),
                  pl.BlockSpec((tm, D_MODEL), row),
                  pl.BlockSpec((tm, D_PLE), row),
                  pl.BlockSpec((D_MODEL, D_MODEL), fixed, **once),
                  pl.BlockSpec((1, D_MODEL), fixed),
                  pl.BlockSpec((D_MODEL, 2 * D_FF), fixed, **once),
                  pl.BlockSpec((D_FF, D_MODEL), fixed, **once),
                  pl.BlockSpec((1, D_MODEL), fixed),
                  pl.BlockSpec((D_MODEL, D_MODEL), fixed, **once),
                  pl.BlockSpec((D_PLE, D_MODEL), fixed, **once)],
        out_specs=pl.BlockSpec((tm, D_MODEL), row),
        out_shape=jax.ShapeDtypeStruct((m, D_MODEL), F32),
        scratch_shapes=[pltpu.VMEM((tm, D_MODEL), F32)],
        compiler_params=_cparams(("arbitrary",)),
        name="tail",
    )(x, o, p, wo, gf, win, wout, gp, wg, wp)


def _diff_lambda(lp, lam_init):
    a = jnp.sum(lp[0:1] * lp[1:2], axis=-1, keepdims=True)
    b = jnp.sum(lp[2:3] * lp[3:4], axis=-1, keepdims=True)
    return jnp.exp(a) - jnp.exp(b) + lam_init


def _online_softmax_step(s, v, m_ref, l_ref, acc_ref):
    m_old = m_ref[...]
    m_new = jnp.maximum(m_old, jnp.max(s, axis=-1, keepdims=True))
    alpha = jnp.exp(m_old - m_new)
    p = jnp.exp(s - m_new)
    l_ref[...] = alpha * l_ref[...] + jnp.sum(p, axis=-1, keepdims=True)
    acc_ref[...] = alpha * acc_ref[...] + _dot(p.astype(BF16), v)
    m_ref[...] = m_new


def _diff_prompt_kernel(qt_ref, kt_ref, lam_ref, sg_ref, q_ref, k_ref, v_ref, o_ref,
                        m1, l1, a1, m2, l2, a2, *, lam_init, tq, tk):
    step = pl.program_id(2)
    qi, ki = qt_ref[step], kt_ref[step]

    @pl.when(ki == 0)
    def _():
        for m, l, a in ((m1, l1, a1), (m2, l2, a2)):
            m[...] = jnp.full(m.shape, NEG, F32)
            l[...] = jnp.zeros(l.shape, F32)
            a[...] = jnp.zeros(a.shape, F32)

    def scores(masked):
        q = q_ref[...]
        lo = _lane_iota((1, LANES)) < HEAD_DIM
        k = k_ref[...]
        s1 = _dot_nt(jnp.where(lo, q, jnp.zeros_like(q)), k)
        s2 = _dot_nt(jnp.where(lo, jnp.zeros_like(q), q), k)
        if masked:
            row = qi * tq + lax.broadcasted_iota(I32, (tq, tk), 0)
            col = ki * tk + lax.broadcasted_iota(I32, (tq, tk), 1)
            ok = col <= row
            s1 = jnp.where(ok, s1, NEG)
            s2 = jnp.where(ok, s2, NEG)
        v = v_ref[...]
        _online_softmax_step(s1, v, m1, l1, a1)
        _online_softmax_step(s2, v, m2, l2, a2)

    straddles = ki * tk + tk - 1 > qi * tq
    pl.when(jnp.logical_not(straddles))(lambda: scores(False))
    pl.when(straddles)(lambda: scores(True))

    @pl.when((ki + 1) * tk >= (qi + 1) * tq)
    def _():
        lam = _diff_lambda(lam_ref[...], lam_init)
        o = a1[...] / l1[...] - lam * (a2[...] / l2[...])
        o_ref[...] = (_rms(o, sg_ref[...]) * (1.0 - lam_init)).astype(BF16)


def _diff_prompt(q, k, v, lam_p, subln, lam_init, batch, tq, tk):
    m = q.shape[0]
    t = m // batch
    nq = t // tq
    pairs = [(a, b) for a in range(nq) for b in range(((a + 1) * tq + tk - 1) // tk)]
    qt = jnp.array([a for a, _ in pairs], I32)
    kt = jnp.array([b for _, b in pairs], I32)
    n_heads = D_MODEL // LANES
    qmap = lambda b, h, s, qt, kt: (b * nq + qt[s], h)
    kmap = lambda b, h, s, qt, kt: (b * (t // tk) + kt[s], h)
    grid_spec = pltpu.PrefetchScalarGridSpec(
        num_scalar_prefetch=2,
        grid=(batch, n_heads, len(pairs)),
        in_specs=[pl.BlockSpec((4, HEAD_DIM), lambda b, h, s, qt, kt: (0, 0)),
                  pl.BlockSpec((1, LANES), lambda b, h, s, qt, kt: (0, 0)),
                  pl.BlockSpec((tq, LANES), qmap),
                  pl.BlockSpec((tk, LANES), kmap),
                  pl.BlockSpec((tk, LANES), kmap)],
        out_specs=pl.BlockSpec((tq, LANES), qmap),
        scratch_shapes=[pltpu.VMEM((tq, 1), F32), pltpu.VMEM((tq, 1), F32), pltpu.VMEM((tq, LANES), F32),
                        pltpu.VMEM((tq, 1), F32), pltpu.VMEM((tq, 1), F32), pltpu.VMEM((tq, LANES), F32)])
    return pl.pallas_call(
        functools.partial(_diff_prompt_kernel, lam_init=lam_init, tq=tq, tk=tk),
        grid_spec=grid_spec,
        out_shape=jax.ShapeDtypeStruct((m, D_MODEL), BF16),
        compiler_params=_cparams(("arbitrary", "arbitrary", "arbitrary")),
        name="diff_prompt",
    )(qt, kt, lam_p, subln, q, k, v)


ROWS = 8


def _tile_rows(x8, n):
    return jnp.concatenate([x8] * n, axis=0)


def _diff_sample_kernel(pt_ref, lam_ref, sg_ref, q_ref, kc_ref, vc_ref, kn_ref, vn_ref, o_ref,
                        qbd_ref, m_ref, l_ref, acc_ref, *, lam_init, n_pages, n_new):
    p = pl.program_id(1)
    n_sub = D_MODEL // HEAD_DIM
    n_rows = n_sub * ROWS
    row = lax.broadcasted_iota(I32, (n_rows, D_MODEL), 0)
    lane = lax.broadcasted_iota(I32, (n_rows, D_MODEL), 1)
    parity, head = row // (n_rows // 2), (row % (n_rows // 2)) // ROWS

    @pl.when(p == 0)
    def _():
        own = lane // HEAD_DIM == 2 * head + parity
        q = _tile_rows(q_ref[...], n_sub)
        qbd_ref[...] = jnp.where(own, q, jnp.zeros_like(q))
        m_ref[...] = jnp.full(m_ref.shape, NEG, F32)
        l_ref[...] = jnp.zeros(l_ref.shape, F32)
        acc_ref[...] = jnp.zeros(acc_ref.shape, F32)

    @pl.when(p < n_pages)
    def _():
        s = _dot_nt(qbd_ref[...], kc_ref[...].astype(BF16))
        _online_softmax_step(s, vc_ref[...].astype(BF16), m_ref, l_ref, acc_ref)

    @pl.when(p == n_pages)
    def _():
        s = _dot_nt(qbd_ref[...], kn_ref[...])
        tok = lax.broadcasted_iota(I32, s.shape, 0) % ROWS
        col = lax.broadcasted_iota(I32, s.shape, 1)
        s = jnp.where((col <= tok) & (col < n_new), s, NEG)
        _online_softmax_step(s, vn_ref[...], m_ref, l_ref, acc_ref)
        own = lane // LANES == head
        full = jnp.where(own, acc_ref[...] / l_ref[...], 0.0)
        full = full.reshape(2, n_sub // 2, ROWS, D_MODEL).sum(axis=1)
        lam = _diff_lambda(lam_ref[...], lam_init)
        o = full[0] - lam * full[1]
        for h in range(D_MODEL // LANES):
            sl = slice(h * LANES, (h + 1) * LANES)
            o_ref[:, sl] = (_rms(o[:, sl], sg_ref[...]) * (1.0 - lam_init)).astype(BF16)


def _diff_sample(q8, cache_k, cache_v, k_new, v_new, pt, page_base, lam_p, subln, lam_init, n_new):
    nb = q8.shape[0]
    n_pages = pt.shape[0] // nb
    n_rows = (D_MODEL // HEAD_DIM) * ROWS
    page = lambda b, p, pt: (page_base + pt[b * n_pages + jnp.minimum(p, n_pages - 1)], 0, 0)
    per_b = lambda b, p, pt: (b, 0, 0)
    fixed = lambda b, p, pt: (0, 0)
    grid_spec = pltpu.PrefetchScalarGridSpec(
        num_scalar_prefetch=1,
        grid=(nb, n_pages + 1),
        in_specs=[pl.BlockSpec((4, HEAD_DIM), fixed),
                  pl.BlockSpec((1, LANES), fixed),
                  pl.BlockSpec((None, ROWS, D_MODEL), per_b),
                  pl.BlockSpec((None, PAGE, D_MODEL), page),
                  pl.BlockSpec((None, PAGE, D_MODEL), page),
                  pl.BlockSpec((None, PAGE, D_MODEL), per_b),
                  pl.BlockSpec((None, PAGE, D_MODEL), per_b)],
        out_specs=pl.BlockSpec((None, ROWS, D_MODEL), per_b),
        scratch_shapes=[pltpu.VMEM((n_rows, D_MODEL), BF16),
                        pltpu.VMEM((n_rows, 1), F32), pltpu.VMEM((n_rows, 1), F32),
                        pltpu.VMEM((n_rows, D_MODEL), F32)])
    return pl.pallas_call(
        functools.partial(_diff_sample_kernel, lam_init=lam_init, n_pages=n_pages, n_new=n_new),
        grid_spec=grid_spec,
        out_shape=jax.ShapeDtypeStruct((nb, ROWS, D_MODEL), BF16),
        compiler_params=_cparams(("arbitrary", "arbitrary")),
        name="diff_sample",
    )(pt, lam_p, subln, q8, cache_k, cache_v, k_new, v_new)


QB = 128
CK = 512


def _dsa_prompt_kernel(qi_ref, tail_ref, ki_ref, q_ref, k_ref, v_ref, o_ref,
                       key_ref, m_ref, l_ref, acc_ref, *, seq):
    i = pl.program_id(1)
    nch = ((i + 1) * QB + CK - 1) // CK
    row = i * QB + lax.broadcasted_iota(I32, (QB, CK), 0)
    col0 = lax.broadcasted_iota(I32, (QB, CK), 1)
    head_w = [tail_ref[:, HEAD_DIM + h:HEAD_DIM + h + 1] for h in range(N_HEADS_IDX)]

    def index_chunk(c, carry):
        kic = ki_ref[pl.ds(pl.multiple_of(c * CK, CK), CK), :]
        sc = jnp.zeros((QB, CK), F32)
        for h in range(N_HEADS_IDX):
            sc = sc + head_w[h] * jnp.maximum(_dot_nt(qi_ref[h], kic), 0.0)
        sc = jnp.where(c * CK + col0 <= row, sc, -jnp.inf)
        key_ref[c] = _sortable(sc)
        return carry

    lax.fori_loop(0, nch, index_chunk, 0)

    def count(pred):
        def body(c, acc):
            hit = pred(key_ref[c], c * CK + col0).astype(I32)
            for j in range(CK // LANES):
                acc = acc + hit[:, j * LANES:(j + 1) * LANES]
            return acc
        acc = lax.fori_loop(0, nch, body, jnp.zeros((QB, LANES), I32))
        return jnp.sum(acc, axis=-1, keepdims=True)

    thr = _kth_largest(lambda t: count(lambda key, col: key >= t), (QB, 1))
    need = TOPK - count(lambda key, col: key > thr)
    n_tie = count(lambda key, col: key == thr)
    all_ties = jnp.max(jnp.where(n_tie > need, 1, 0)) == 0
    nbits = (seq - 1).bit_length() + 1
    cut = lax.cond(
        all_ties,
        lambda: jnp.full((QB, 1), 2 ** nbits - 1, I32),
        lambda: _tie_cutoff(lambda j: count(lambda key, col: (key == thr) & (col < j)), need, (QB, 1), nbits))

    m_ref[...] = jnp.full(m_ref.shape, NEG, F32)
    l_ref[...] = jnp.zeros(l_ref.shape, F32)
    acc_ref[...] = jnp.zeros(acc_ref.shape, F32)
    lo = _lane_iota((1, LANES)) < HEAD_DIM

    def attend_chunk(c, carry):
        key = key_ref[c]
        col = c * CK + col0
        sel = ((key > thr) | ((key == thr) & (col < cut))) & (col <= row)
        off = pl.multiple_of(c * CK, CK)
        for hp in range(D_MODEL // LANES):
            sl = slice(hp * LANES, (hp + 1) * LANES)
            q = q_ref[:, sl]
            k = k_ref[pl.ds(off, CK), sl]
            v = v_ref[pl.ds(off, CK), sl]
            s1 = jnp.where(sel, _dot_nt(jnp.where(lo, q, jnp.zeros_like(q)), k), NEG)
            s2 = jnp.where(sel, _dot_nt(jnp.where(lo, jnp.zeros_like(q), q), k), NEG)
            m_old = m_ref[:, sl]
            m_new = jnp.maximum(m_old, jnp.where(lo, jnp.max(s1, axis=-1, keepdims=True),
                                                 jnp.max(s2, axis=-1, keepdims=True)))
            alpha = jnp.exp(m_old - m_new)
            p1 = jnp.exp(s1 - m_new[:, 0:1])
            p2 = jnp.exp(s2 - m_new[:, HEAD_DIM:HEAD_DIM + 1])
            l_ref[:, sl] = alpha * l_ref[:, sl] + jnp.where(
                lo, jnp.sum(p1, axis=-1, keepdims=True), jnp.sum(p2, axis=-1, keepdims=True))
            acc_ref[:, sl] = alpha * acc_ref[:, sl] + jnp.where(
                lo, _dot(p1.astype(BF16), v), _dot(p2.astype(BF16), v))
            m_ref[:, sl] = m_new
        return carry

    lax.fori_loop(0, nch, attend_chunk, 0)
    o_ref[...] = (acc_ref[...] / l_ref[...]).astype(BF16)


def _dsa_prompt(qi, tail, ki, q, k, v, batch):
    m = q.shape[0]
    t = m // batch
    nq = t // QB
    assert t % CK == 0 and CK % QB == 0
    blk = lambda b, i: (b * nq + i, 0)
    per_b = lambda b, i: (b, 0)
    return pl.pallas_call(
        functools.partial(_dsa_prompt_kernel, seq=t),
        grid=(batch, nq),
        in_specs=[pl.BlockSpec((N_HEADS_IDX, QB, HEAD_DIM), lambda b, i: (0, b * nq + i, 0)),
                  pl.BlockSpec((QB, LANES), blk),
                  pl.BlockSpec((t, HEAD_DIM), per_b),
                  pl.BlockSpec((QB, D_MODEL), blk),
                  pl.BlockSpec((t, D_MODEL), per_b),
                  pl.BlockSpec((t, D_MODEL), per_b)],
        out_specs=pl.BlockSpec((QB, D_MODEL), blk),
        out_shape=jax.ShapeDtypeStruct((m, D_MODEL), BF16),
        scratch_shapes=[pltpu.VMEM(((t + CK - 1) // CK, QB, CK), I32),
                        pltpu.VMEM((QB, D_MODEL), F32), pltpu.VMEM((QB, D_MODEL), F32),
                        pltpu.VMEM((QB, D_MODEL), F32)],
        compiler_params=_cparams(("arbitrary", "arbitrary")),
        name="dsa_prompt",
    )(qi, tail, ki, q, k, v)


def _dsa_index_sample_kernel(pt_ref, qi_ref, wi_ref, kc_ref, kn_ref, key_out, thr_out, cut_out,
                             key_ref, *, n_pages, n_new):
    p = pl.program_id(1)

    def page_scores(kpage):
        s = jnp.maximum(_dot_nt(qi_ref[...], kpage), 0.0) * wi_ref[...]
        return s.reshape(N_HEADS_IDX, ROWS, PAGE).sum(axis=0)

    @pl.when(p < n_pages)
    def _():
        key = _sortable(page_scores(kc_ref[...].astype(BF16)))
        key_ref[p] = key
        key_out[...] = key

    @pl.when(p == n_pages)
    def _():
        sc = page_scores(kn_ref[...])
        tok = lax.broadcasted_iota(I32, sc.shape, 0)
        col = lax.broadcasted_iota(I32, sc.shape, 1)
        key = _sortable(jnp.where((col <= tok) & (col < n_new), sc, -jnp.inf))
        key_ref[p] = key
        key_out[...] = key
        keys = key_ref[...]
        idx = (lax.broadcasted_iota(I32, keys.shape, 0) * PAGE + lax.broadcasted_iota(I32, keys.shape, 2))

        def count(pred):
            return jnp.sum(jnp.sum(pred.astype(I32), axis=0), axis=-1, keepdims=True)

        thr = _kth_largest(lambda t: count(keys >= t[None]), (ROWS, 1))
        need = TOPK - count(keys > thr[None])
        nbits = ((n_pages + 1) * PAGE - 1).bit_length() + 1
        cut = _tie_cutoff(lambda j: count((keys == thr[None]) & (idx < j[None])), need, (ROWS, 1), nbits)
        thr_out[...] = jnp.broadcast_to(thr, (ROWS, LANES))
        cut_out[...] = jnp.broadcast_to(cut, (ROWS, LANES))


def _dsa_index_sample(qi, wi, cache_ki, ki_new, pt, page_base, n_new):
    nb = qi.shape[0]
    n_pages = pt.shape[0] // nb
    page = lambda b, p, pt: (page_base + pt[b * n_pages + jnp.minimum(p, n_pages - 1)], 0, 0)
    per_b = lambda b, p, pt: (b, 0, 0)
    grid_spec = pltpu.PrefetchScalarGridSpec(
        num_scalar_prefetch=1,
        grid=(nb, n_pages + 1),
        in_specs=[pl.BlockSpec((None, N_HEADS_IDX * ROWS, HEAD_DIM), per_b),
                  pl.BlockSpec((None, N_HEADS_IDX * ROWS, PAGE), per_b),
                  pl.BlockSpec((None, PAGE, HEAD_DIM), page),
                  pl.BlockSpec((None, PAGE, HEAD_DIM), per_b)],
        out_specs=[pl.BlockSpec((None, None, ROWS, PAGE), lambda b, p, pt: (b, p, 0, 0)),
                   pl.BlockSpec((None, ROWS, LANES), per_b),
                   pl.BlockSpec((None, ROWS, LANES), per_b)],
        scratch_shapes=[pltpu.VMEM((n_pages + 1, ROWS, PAGE), I32)])
    return pl.pallas_call(
        functools.partial(_dsa_index_sample_kernel, n_pages=n_pages, n_new=n_new),
        grid_spec=grid_spec,
        out_shape=[jax.ShapeDtypeStruct((nb, n_pages + 1, ROWS, PAGE), I32),
                   jax.ShapeDtypeStruct((nb, ROWS, LANES), I32),
                   jax.ShapeDtypeStruct((nb, ROWS, LANES), I32)],
        compiler_params=_cparams(("arbitrary", "arbitrary")),
        name="dsa_index_sample",
    )(pt, qi, wi, cache_ki, ki_new)


def _dsa_sample_kernel(pt_ref, q_ref, key_in, thr_ref, cut_ref, kc_ref, vc_ref, kn_ref, vn_ref, o_ref,
                       qbd_ref, m_ref, l_ref, acc_ref, *, n_pages, n_new):
    p = pl.program_id(1)
    n_heads = D_MODEL // HEAD_DIM
    n_rows = n_heads * ROWS
    row = lax.broadcasted_iota(I32, (n_rows, D_MODEL), 0)
    lane = lax.broadcasted_iota(I32, (n_rows, D_MODEL), 1)
    own = lane // HEAD_DIM == row // ROWS

    @pl.when(p == 0)
    def _():
        q = _tile_rows(q_ref[...], n_heads)
        qbd_ref[...] = jnp.where(own, q, jnp.zeros_like(q))
        m_ref[...] = jnp.full(m_ref.shape, NEG, F32)
        l_ref[...] = jnp.zeros(l_ref.shape, F32)
        acc_ref[...] = jnp.zeros(acc_ref.shape, F32)

    def attend(k, v, new_page):
        key = key_in[...]
        col = lax.broadcasted_iota(I32, key.shape, 1)
        thr = thr_ref[...]
        sel = (key > thr) | ((key == thr) & (p * PAGE + col < cut_ref[...]))
        if new_page:
            tok = lax.broadcasted_iota(I32, key.shape, 0)
            sel = sel & (col <= tok) & (col < n_new)
        sel = _tile_rows(jnp.where(sel, 1, 0), n_heads) > 0
        s = jnp.where(sel, _dot_nt(qbd_ref[...], k), NEG)
        _online_softmax_step(s, v, m_ref, l_ref, acc_ref)

    @pl.when(p < n_pages)
    def _():
        attend(kc_ref[...].astype(BF16), vc_ref[...].astype(BF16), False)

    @pl.when(p == n_pages)
    def _():
        attend(kn_ref[...], vn_ref[...], True)
        full = jnp.where(own, acc_ref[...] / l_ref[...], 0.0)
        o_ref[...] = full.reshape(n_heads, ROWS, D_MODEL).sum(axis=0).astype(BF16)


def _dsa_sample(q8, keys, thr, cut, cache_k, cache_v, k_new, v_new, pt, page_base, n_new):
    nb = q8.shape[0]
    n_pages = pt.shape[0] // nb
    n_rows = (D_MODEL // HEAD_DIM) * ROWS
    page = lambda b, p, pt: (page_base + pt[b * n_pages + jnp.minimum(p, n_pages - 1)], 0, 0)
    per_b = lambda b, p, pt: (b, 0, 0)
    grid_spec = pltpu.PrefetchScalarGridSpec(
        num_scalar_prefetch=1,
        grid=(nb, n_pages + 1),
        in_specs=[pl.BlockSpec((None, ROWS, D_MODEL), per_b),
                  pl.BlockSpec((None, None, ROWS, PAGE), lambda b, p, pt: (b, p, 0, 0)),
                  pl.BlockSpec((None, ROWS, LANES), per_b),
                  pl.BlockSpec((None, ROWS, LANES), per_b),
                  pl.BlockSpec((None, PAGE, D_MODEL), page),
                  pl.BlockSpec((None, PAGE, D_MODEL), page),
                  pl.BlockSpec((None, PAGE, D_MODEL), per_b),
                  pl.BlockSpec((None, PAGE, D_MODEL), per_b)],
        out_specs=pl.BlockSpec((None, ROWS, D_MODEL), per_b),
        scratch_shapes=[pltpu.VMEM((n_rows, D_MODEL), BF16),
                        pltpu.VMEM((n_rows, 1), F32), pltpu.VMEM((n_rows, 1), F32),
                        pltpu.VMEM((n_rows, D_MODEL), F32)])
    return pl.pallas_call(
        functools.partial(_dsa_sample_kernel, n_pages=n_pages, n_new=n_new),
        grid_spec=grid_spec,
        out_shape=jax.ShapeDtypeStruct((nb, ROWS, D_MODEL), BF16),
        compiler_params=_cparams(("arbitrary", "arbitrary")),
        name="dsa_sample",
    )(pt, q8, keys, thr, cut, cache_k, cache_v, k_new, v_new)


def _pad_rows(a, rows):
    return jnp.pad(a, ((0, 0), (0, rows - a.shape[1]), (0, 0)))


def _tile_gain(g):
    return jnp.tile(g.astype(F32), LANES // HEAD_DIM)[None, :]


def kernel(x_prompt, x_sample, cache_a_k, cache_a_v, cache_b_k, cache_b_v, cache_b_kidx, page_table,
           p_prompt, p_sample, norm_mix, norm_ffn, norm_ple, a_w_qkv, a_w_o, a_q_norm, a_k_norm, a_lambda,
           a_subln, b_w_qkv, b_w_o, b_q_norm, b_k_norm, b_kidx_norm, w_ffn_in, w_ffn_out, w_ple_gate,
           w_ple_proj):
    batch, seq, _ = x_prompt.shape
    nb, n_new, _ = x_sample.shape
    depth = norm_mix.shape[0]
    n_pool, page_size = cache_a_k.shape[1], cache_a_k.shape[2]
    n_pages = page_table.shape[1]
    past = n_pages * page_size
    mp, ms = batch * seq, nb * n_new
    tm_p = min(256, mp)

    xp = x_prompt.reshape(mp, D_MODEL)
    xs = x_sample.reshape(ms, D_MODEL)
    pt = page_table.reshape(-1).astype(I32)
    cos_p, sin_p = _rope_tables(jnp.arange(seq, dtype=I32))
    cos_s, sin_s = _rope_tables(jnp.tile(past + jnp.arange(n_new, dtype=I32), nb))
    ca_k = cache_a_k.reshape(-1, page_size, D_MODEL)
    ca_v = cache_a_v.reshape(-1, page_size, D_MODEL)
    cb_k = cache_b_k.reshape(-1, page_size, D_MODEL)
    cb_v = cache_b_v.reshape(-1, page_size, D_MODEL)
    cb_ki = cache_b_kidx.reshape(-1, page_size, HEAD_DIM)
    row_vec = lambda g: g.astype(F32)[None, :]

    def sample_rows(a, rows):
        return _pad_rows(a.reshape(nb, n_new, a.shape[-1]), rows)

    outs = {name: [] for name in ("akp", "avp", "bkp", "bvp", "bkip", "aks", "avs", "bks", "bvs", "bkis")}
    for i in range(depth):
        j = i // 2
        gn = row_vec(norm_mix[i])
        if i % 2 == 0:
            lam_init = 0.8 - 0.6 * math.exp(-0.3 * i)
            w = a_w_qkv[j].astype(BF16)
            qg, kg = _tile_gain(a_q_norm[j]), _tile_gain(a_k_norm[j])
            lam_p, subln = a_lambda[j].astype(F32), row_vec(a_subln[j])
            wo = a_w_o[j].astype(BF16)
            q, k32, k16, v32, v16 = _proj_a(xp, gn, w, qg, kg, cos_p, sin_p, tm_p)
            op = _diff_prompt(q, k16, v16, lam_p, subln, lam_init, batch, min(512, seq), min(512, seq))
            outs["akp"].append(k32)
            outs["avp"].append(v32)
            q, k32, k16, v32, v16 = _proj_a(xs, gn, w, qg, kg, cos_s, sin_s, ms)
            os_ = _diff_sample(sample_rows(q, ROWS), ca_k, ca_v, sample_rows(k16, PAGE), sample_rows(v16, PAGE),
                               pt, j * n_pool, lam_p, subln, lam_init, n_new)
            outs["aks"].append(k32)
            outs["avs"].append(v32)
        else:
            w = b_w_qkv[j].astype(BF16)
            wt = jnp.pad(w[:, B_TAIL:], ((0, 0), (0, LANES - (w.shape[1] - B_TAIL))))
            w = w[:, :B_TAIL]
            qg, kg, kig = _tile_gain(b_q_norm[j]), _tile_gain(b_k_norm[j]), _tile_gain(b_kidx_norm[j])
            wo = b_w_o[j].astype(BF16)
            q, k32, k16, v32, v16, qi, tail, ki16 = _proj_b(xp, gn, w, wt, qg, kg, kig, cos_p, sin_p, tm_p)
            op = _dsa_prompt(qi, tail, ki16, q, k16, v16, batch)
            outs["bkp"].append(k32)
            outs["bvp"].append(v32)
            outs["bkip"].append(tail[:, :HEAD_DIM])
            q, k32, k16, v32, v16, qi, tail, ki16 = _proj_b(xs, gn, w, wt, qg, kg, kig, cos_s, sin_s, ms)
            qi8 = jnp.pad(qi.reshape(N_HEADS_IDX, nb, n_new, HEAD_DIM), ((0, 0), (0, 0), (0, ROWS - n_new), (0, 0)))
            qi8 = qi8.transpose(1, 0, 2, 3).reshape(nb, N_HEADS_IDX * ROWS, HEAD_DIM)
            wi = tail[:, HEAD_DIM:HEAD_DIM + N_HEADS_IDX].reshape(nb, n_new, N_HEADS_IDX)
            wi8 = jnp.pad(wi, ((0, 0), (0, ROWS - n_new), (0, 0))).transpose(0, 2, 1)
            wi8 = jnp.broadcast_to(wi8.reshape(nb, N_HEADS_IDX * ROWS, 1), (nb, N_HEADS_IDX * ROWS, PAGE))
            keys, thr, cut = _dsa_index_sample(qi8, wi8, cb_ki, sample_rows(ki16, PAGE), pt, j * n_pool, n_new)
            os_ = _dsa_sample(sample_rows(q, ROWS), keys, thr, cut, cb_k, cb_v,
                              sample_rows(k16, PAGE), sample_rows(v16, PAGE), pt, j * n_pool, n_new)
            outs["bks"].append(k32)
            outs["bvs"].append(v32)
            outs["bkis"].append(tail[:, :HEAD_DIM])
        os_ = os_[:, :n_new].reshape(ms, D_MODEL)
        tail_w = (wo, row_vec(norm_ffn[i]), w_ffn_in[i].astype(BF16), w_ffn_out[i].astype(BF16),
                  row_vec(norm_ple[i]), w_ple_gate[i].astype(BF16), w_ple_proj[i].astype(BF16))
        xp = _tail(xp, op, p_prompt[i].reshape(mp, D_PLE), *tail_w, tm=tm_p)
        xs = _tail(xs, os_, p_sample[i].reshape(ms, D_PLE), *tail_w, tm=ms)

    n_a, n_b = D_MODEL // (2 * HEAD_DIM), D_MODEL // HEAD_DIM
    st = lambda name, shape: jnp.stack(outs[name]).reshape((len(outs[name]),) + shape)
    return (xp.reshape(batch, seq, D_MODEL), xs.reshape(nb, n_new, D_MODEL),
            st("akp", (batch, seq, 2 * n_a, HEAD_DIM)), st("avp", (batch, seq, n_a, 2 * HEAD_DIM)),
            st("bkp", (batch, seq, n_b, HEAD_DIM)), st("bvp", (batch, seq, n_b, HEAD_DIM)),
            st("bkip", (batch, seq, HEAD_DIM)),
            st("aks", (nb, n_new, 2 * n_a, HEAD_DIM)), st("avs", (nb, n_new, n_a, 2 * HEAD_DIM)),
            st("bks", (nb, n_new, n_b, HEAD_DIM)), st("bvs", (nb, n_new, n_b, HEAD_DIM)),
            st("bkis", (nb, n_new, HEAD_DIM)))
```

```python
import functools
import math

import jax
import jax.numpy as jnp
from jax import lax
from jax.experimental import pallas as pl
from jax.experimental.pallas import tpu as pltpu

F32 = jnp.float32
BF16 = jnp.bfloat16
I32 = jnp.int32

D_MODEL = 1024
HEAD_DIM = 64
LANES = 128
N_HEADS_IDX = 8
TOPK = 256
ROPE_THETA = 10000.0
RMS_EPS = 1e-6
D_FF = 2816
D_PLE = 256
PAGE = 128
NEG = -1e30
INT_MIN = -(2 ** 31)
VMEM_LIMIT = 56 * 1024 * 1024
Q_SCALE = HEAD_DIM ** -0.5 * math.log2(math.e)

_NT = (((1,), (1,)), ((), ()))


def _dot(a, b):
    return jnp.dot(a, b, preferred_element_type=F32)


def _dot_nt(a, b):
    return lax.dot_general(a, b, _NT, preferred_element_type=F32)


def _cparams(sem):
    return pltpu.CompilerParams(dimension_semantics=sem, vmem_limit_bytes=VMEM_LIMIT)


def _rms(x, g):
    return x * lax.rsqrt(jnp.mean(x * x, axis=-1, keepdims=True) + RMS_EPS) * g


def _lane_iota(shape):
    return lax.broadcasted_iota(I32, shape, len(shape) - 1)


def _head_norm(y, gmat, g):
    ss = _dot((y * y).astype(BF16), gmat)
    return y * lax.rsqrt(ss * (1.0 / HEAD_DIM) + RMS_EPS) * g


def _rope(y, cos, sin, first_half):
    rot = jnp.where(first_half, pltpu.roll(y, LANES - HEAD_DIM // 2, 1), pltpu.roll(y, HEAD_DIM // 2, 1))
    return y * cos + rot * sin


def _rope_tables(pos):
    half = HEAD_DIM // 2
    inv_freq = ROPE_THETA ** (-jnp.arange(half, dtype=F32) * 2.0 / HEAD_DIM)
    ang = pos.astype(F32)[:, None] * inv_freq[None, :]
    cos, sin = jnp.cos(ang), jnp.sin(ang)
    return (jnp.concatenate([cos, cos, cos, cos], axis=-1),
            jnp.concatenate([-sin, sin, -sin, sin], axis=-1))


def _group_matrix():
    r = lax.broadcasted_iota(I32, (LANES, LANES), 0) // HEAD_DIM
    c = lax.broadcasted_iota(I32, (LANES, LANES), 1) // HEAD_DIM
    return (r == c).astype(BF16)


def _sortable(score):
    score = jnp.where(score == 0.0, 0.0, score)
    bits = lax.bitcast_convert_type(score, I32)
    return bits ^ ((bits >> 31) & 0x7FFFFFFF)


def _kth_largest(count_ge, shape):
    t0 = jnp.where(count_ge(jnp.zeros(shape, I32)) >= TOPK, 0, INT_MIN).astype(I32)

    def body(b, t):
        cand = t | (1 << (30 - b))
        return jnp.where(count_ge(cand) >= TOPK, cand, t)

    return lax.fori_loop(0, 31, body, t0)


def _tie_cutoff(count_tie_lt, need, shape, nbits):
    def body(b, j):
        cand = j | (1 << (nbits - 1 - b))
        return jnp.where(count_tie_lt(cand) <= need, cand, j)

    return lax.fori_loop(0, nbits, body, jnp.zeros(shape, I32))


def _qkv_heads(h, w_ref, qg_ref, kg_ref, cos, sin, gmat, first, q_ref, k32_ref, k16_ref, v32_ref, v16_ref):
    yq = _dot(h, w_ref[:, 0:D_MODEL])
    yk = _dot(h, w_ref[:, D_MODEL:2 * D_MODEL])
    for c in range(D_MODEL // LANES):
        sl = slice(c * LANES, (c + 1) * LANES)
        q = _rope(_head_norm(yq[:, sl], gmat, qg_ref[...]), cos, sin, first)
        q_ref[:, sl] = (q * Q_SCALE).astype(BF16)
        k = _rope(_head_norm(yk[:, sl], gmat, kg_ref[...]), cos, sin, first)
        k32_ref[:, sl] = k
        k16_ref[:, sl] = k.astype(BF16)
    v = _dot(h, w_ref[:, 2 * D_MODEL:3 * D_MODEL])
    v32_ref[...] = v
    v16_ref[...] = v.astype(BF16)


def _proj_a_kernel(x_ref, gn_ref, w_ref, qg_ref, kg_ref, cos_ref, sin_ref, gmat_ref,
                   q_ref, k32_ref, k16_ref, v32_ref, v16_ref):
    h = _rms(x_ref[...], gn_ref[...]).astype(BF16)
    first = (_lane_iota((1, LANES)) % HEAD_DIM) < HEAD_DIM // 2
    _qkv_heads(h, w_ref, qg_ref, kg_ref, cos_ref[...], sin_ref[...], gmat_ref[...], first,
               q_ref, k32_ref, k16_ref, v32_ref, v16_ref)


def _proj_a(x, gn, w, qg, kg, cos, sin, tm):
    m = x.shape[0]
    nt = cos.shape[0] // tm
    row = lambda i: (i, 0)
    fixed = lambda i: (0, 0)
    wide = pl.BlockSpec((tm, D_MODEL), row)
    return pl.pallas_call(
        _proj_a_kernel,
        grid=(m // tm,),
        in_specs=[wide,
                  pl.BlockSpec((1, D_MODEL), fixed),
                  pl.BlockSpec((D_MODEL, 3 * D_MODEL), fixed),
                  pl.BlockSpec((1, LANES), fixed),
                  pl.BlockSpec((1, LANES), fixed),
                  pl.BlockSpec((tm, LANES), lambda i: (i % nt, 0)),
                  pl.BlockSpec((tm, LANES), lambda i: (i % nt, 0)),
                  pl.BlockSpec((LANES, LANES), fixed)],
        out_specs=[wide, wide, wide, wide, wide],
        out_shape=[jax.ShapeDtypeStruct((m, D_MODEL), BF16),
                   jax.ShapeDtypeStruct((m, D_MODEL), F32),
                   jax.ShapeDtypeStruct((m, D_MODEL), BF16),
                   jax.ShapeDtypeStruct((m, D_MODEL), F32),
                   jax.ShapeDtypeStruct((m, D_MODEL), BF16)],
        compiler_params=_cparams(("arbitrary",)),
        name="proj_a",
    )(x, gn, w, qg, kg, cos, sin, _group_matrix())


B_QI = 3 * D_MODEL
B_TAIL = B_QI + N_HEADS_IDX * HEAD_DIM


def _proj_b_kernel(x_ref, gn_ref, w_ref, wt_ref, qg_ref, kg_ref, kig_ref, cos_ref, sin_ref, gmat_ref,
                   q_ref, k32_ref, k16_ref, v32_ref, v16_ref, qi_ref, tail_ref, ki16_ref):
    h = _rms(x_ref[...], gn_ref[...]).astype(BF16)
    cos, sin = cos_ref[...], sin_ref[...]
    lane = _lane_iota((1, LANES))
    first = (lane % HEAD_DIM) < HEAD_DIM // 2
    _qkv_heads(h, w_ref, qg_ref, kg_ref, cos, sin, gmat_ref[...], first,
               q_ref, k32_ref, k16_ref, v32_ref, v16_ref)
    yi = _dot(h, w_ref[:, B_QI:B_TAIL])
    for c in range(N_HEADS_IDX * HEAD_DIM // LANES):
        qi = _rope(yi[:, c * LANES:(c + 1) * LANES], cos, sin, first) * HEAD_DIM ** -0.5
        qi_ref[2 * c] = qi[:, :HEAD_DIM].astype(BF16)
        qi_ref[2 * c + 1] = qi[:, HEAD_DIM:].astype(BF16)
    yt = _dot(h, wt_ref[...])
    is_key = lane < HEAD_DIM
    ms = jnp.sum(jnp.where(is_key, yt * yt, 0.0), axis=-1, keepdims=True) * (1.0 / HEAD_DIM)
    ki = _rope(yt * lax.rsqrt(ms + RMS_EPS) * kig_ref[...], cos, sin, first)
    tail = jnp.where(is_key, ki, yt * N_HEADS_IDX ** -0.5)
    tail_ref[...] = tail
    ki16_ref[...] = tail[:, :HEAD_DIM].astype(BF16)


def _proj_b(x, gn, w, wt, qg, kg, kig, cos, sin, tm):
    m = x.shape[0]
    nt = cos.shape[0] // tm
    row = lambda i: (i, 0)
    fixed = lambda i: (0, 0)
    wide = pl.BlockSpec((tm, D_MODEL), row)
    return pl.pallas_call(
        _proj_b_kernel,
        grid=(m // tm,),
        in_specs=[wide,
                  pl.BlockSpec((1, D_MODEL), fixed),
                  pl.BlockSpec((D_MODEL, B_TAIL), fixed),
                  pl.BlockSpec((D_MODEL, LANES), fixed),
                  pl.BlockSpec((1, LANES), fixed),
                  pl.BlockSpec((1, LANES), fixed),
                  pl.BlockSpec((1, LANES), fixed),
                  pl.BlockSpec((tm, LANES), lambda i: (i % nt, 0)),
                  pl.BlockSpec((tm, LANES), lambda i: (i % nt, 0)),
                  pl.BlockSpec((LANES, LANES), fixed)],
        out_specs=[wide, wide, wide, wide, wide,
                   pl.BlockSpec((N_HEADS_IDX, tm, HEAD_DIM), lambda i: (0, i, 0)),
                   pl.BlockSpec((tm, LANES), row),
                   pl.BlockSpec((tm, HEAD_DIM), row)],
        out_shape=[jax.ShapeDtypeStruct((m, D_MODEL), BF16),
                   jax.ShapeDtypeStruct((m, D_MODEL), F32),
                   jax.ShapeDtypeStruct((m, D_MODEL), BF16),
                   jax.ShapeDtypeStruct((m, D_MODEL), F32),
                   jax.ShapeDtypeStruct((m, D_MODEL), BF16),
                   jax.ShapeDtypeStruct((N_HEADS_IDX, m, HEAD_DIM), BF16),
                   jax.ShapeDtypeStruct((m, LANES), F32),
                   jax.ShapeDtypeStruct((m, HEAD_DIM), BF16)],
        compiler_params=_cparams(("arbitrary",)),
        name="proj_b",
    )(x, gn, w, wt, qg, kg, kig, cos, sin, _group_matrix())


FF_CHUNK = 256


def _tail_kernel(x_ref, o_ref, p_ref, wo_ref, gf_ref, win_ref, wout_ref, gp_ref, wg_ref, wp_ref,
                 y_ref, acc_ref):
    x1 = x_ref[...] + _dot(o_ref[...], wo_ref[...])
    h = _rms(x1, gf_ref[...]).astype(BF16)
    for c in range(D_FF // FF_CHUNK):
        g = _dot(h, win_ref[:, c * FF_CHUNK:(c + 1) * FF_CHUNK])
        u = _dot(h, win_ref[:, D_FF + c * FF_CHUNK:D_FF + (c + 1) * FF_CHUNK])
        a = (g * jax.nn.sigmoid(g) * u).astype(BF16)
        d = _dot(a, wout_ref[c * FF_CHUNK:(c + 1) * FF_CHUNK, :])
        if c == 0:
            acc_ref[...] = x1 + d
        else:
            acc_ref[...] += d
    x2 = acc_ref[...]
    gate = jax.nn.sigmoid(_dot(_rms(x2, gp_ref[...]).astype(BF16), wg_ref[...]))
    y_ref[...] = x2 + gate * _dot(p_ref[...].astype(BF16), wp_ref[...])


def _tail(x, o, p, wo, gf, win, wout, gp, wg, wp, tm):
    m = x.shape[0]
    row = lambda i: (i, 0)
    fixed = lambda i: (0, 0)
    once = dict(pipeline_mode=pl.Buffered(1))
    return pl.pallas_call(
        _tail_kernel,
        grid=(m // tm,),
        in_specs=[pl.BlockSpec((tm, D_MODEL), row),
                  pl.BlockSpec((tm, D_MODEL), row),
                  pl.BlockSpec((tm, D_PLE), row),
                  pl.BlockSpec((D_MODEL, D_MODEL), fixed, **once),
                  pl.BlockSpec((1, D_MODEL), fixed),
                  pl.BlockSpec((D_MODEL, 2 * D_FF), fixed, **once),
                  pl.BlockSpec((D_FF, D_MODEL), fixed, **once),
                  pl.BlockSpec((1, D_MODEL), fixed),
                  pl.BlockSpec((D_MODEL, D_MODEL), fixed, **once),
                  pl.BlockSpec((D_PLE, D_MODEL), fixed, **once)],
        out_specs=pl.BlockSpec((tm, D_MODEL), row),
        out_shape=jax.ShapeDtypeStruct((m, D_MODEL), F32),
        scratch_shapes=[pltpu.VMEM((tm, D_MODEL), F32)],
        compiler_params=_cparams(("arbitrary",)),
        name="tail",
    )(x, o, p, wo, gf, win, wout, gp, wg, wp)


def _diff_lambda(lp, lam_init):
    a = jnp.sum(lp[0:1] * lp[1:2], axis=-1, keepdims=True)
    b = jnp.sum(lp[2:3] * lp[3:4], axis=-1, keepdims=True)
    return jnp.exp(a) - jnp.exp(b) + lam_init


def _softmax_update(s, m_ref, l_ref):
    m_old = m_ref[...]
    m_new = jnp.maximum(m_old, jnp.max(s, axis=-1, keepdims=True))
    alpha = jnp.exp2(m_old - m_new)
    pr = jnp.exp2(s - m_new)
    l_ref[...] = alpha * l_ref[...] + jnp.sum(pr, axis=-1, keepdims=True)
    m_ref[...] = m_new
    return alpha, pr


def _init_softmax(m_ref, l_ref, acc_ref):
    m_ref[...] = jnp.full(m_ref.shape, NEG, F32)
    l_ref[...] = jnp.zeros(l_ref.shape, F32)
    acc_ref[...] = jnp.zeros(acc_ref.shape, F32)


def _diff_prompt_kernel(qt_ref, kt_ref, lam_ref, sg_ref, q_ref, k_ref, v_ref, o_ref,
                        m1, l1, a1, m2, l2, a2, *, lam_init, tq, tk):
    step = pl.program_id(2)
    qi, ki = qt_ref[step], kt_ref[step]

    @pl.when(ki == 0)
    def _():
        _init_softmax(m1, l1, a1)
        _init_softmax(m2, l2, a2)

    def scores(masked):
        q = q_ref[...]
        lo = _lane_iota((1, LANES)) < HEAD_DIM
        k = k_ref[...]
        s1 = _dot_nt(jnp.where(lo, q, jnp.zeros_like(q)), k)
        s2 = _dot_nt(jnp.where(lo, jnp.zeros_like(q), q), k)
        if masked:
            row = qi * tq + lax.broadcasted_iota(I32, (tq, tk), 0)
            col = ki * tk + lax.broadcasted_iota(I32, (tq, tk), 1)
            ok = col <= row
            s1 = jnp.where(ok, s1, NEG)
            s2 = jnp.where(ok, s2, NEG)
        v = v_ref[...]
        for s, m, l, a in ((s1, m1, l1, a1), (s2, m2, l2, a2)):
            alpha, pr = _softmax_update(s, m, l)
            a[...] = alpha * a[...] + _dot(pr.astype(BF16), v)

    straddles = ki * tk + tk - 1 > qi * tq
    pl.when(jnp.logical_not(straddles))(lambda: scores(False))
    pl.when(straddles)(lambda: scores(True))

    @pl.when((ki + 1) * tk >= (qi + 1) * tq)
    def _():
        lam = _diff_lambda(lam_ref[...], lam_init)
        o = a1[...] / l1[...] - lam * (a2[...] / l2[...])
        o_ref[...] = (_rms(o, sg_ref[...]) * (1.0 - lam_init)).astype(BF16)


def _diff_prompt(q, k, v, lam_p, subln, lam_init, batch, tq, tk):
    m = q.shape[0]
    t = m // batch
    nq = t // tq
    pairs = [(a, b) for a in range(nq) for b in range(((a + 1) * tq + tk - 1) // tk)]
    qt = jnp.array([a for a, _ in pairs], I32)
    kt = jnp.array([b for _, b in pairs], I32)
    n_heads = D_MODEL // LANES
    qmap = lambda b, h, s, qt, kt: (b * nq + qt[s], h)
    kmap = lambda b, h, s, qt, kt: (b * (t // tk) + kt[s], h)
    grid_spec = pltpu.PrefetchScalarGridSpec(
        num_scalar_prefetch=2,
        grid=(batch, n_heads, len(pairs)),
        in_specs=[pl.BlockSpec((4, HEAD_DIM), lambda b, h, s, qt, kt: (0, 0)),
                  pl.BlockSpec((1, LANES), lambda b, h, s, qt, kt: (0, 0)),
                  pl.BlockSpec((tq, LANES), qmap),
                  pl.BlockSpec((tk, LANES), kmap),
                  pl.BlockSpec((tk, LANES), kmap)],
        out_specs=pl.BlockSpec((tq, LANES), qmap),
        scratch_shapes=[pltpu.VMEM((tq, 1), F32), pltpu.VMEM((tq, 1), F32), pltpu.VMEM((tq, LANES), F32),
                        pltpu.VMEM((tq, 1), F32), pltpu.VMEM((tq, 1), F32), pltpu.VMEM((tq, LANES), F32)])
    return pl.pallas_call(
        functools.partial(_diff_prompt_kernel, lam_init=lam_init, tq=tq, tk=tk),
        grid_spec=grid_spec,
        out_shape=jax.ShapeDtypeStruct((m, D_MODEL), BF16),
        compiler_params=_cparams(("arbitrary", "arbitrary", "arbitrary")),
        name="diff_prompt",
    )(qt, kt, lam_p, subln, q, k, v)


ROWS = 8
GP = 4
GI = 16


def _page_map(g, group, layer, n_pages, ndim):
    def index_map(b, p, pt):
        page = pt[b * n_pages + jnp.minimum(p * group + g, n_pages - 1)]
        return (layer, page) + (0,) * ndim
    return index_map


def _page_specs(block, group, layer, n_pages):
    return [pl.BlockSpec((None, None) + tuple(block), _page_map(g, group, layer, n_pages, len(block)))
            for g in range(group)]


def _new_token_mask(shape, n_new):
    tok = lax.broadcasted_iota(I32, shape, 0) % ROWS
    col = lax.broadcasted_iota(I32, shape, 1)
    return (col <= tok) & (col < n_new)


def _diff_sample_kernel(pt_ref, lam_ref, sg_ref, qbd_ref, expand_ref, *refs, lam_init, n_steps, n_new):
    k_refs, v_refs = refs[:GP], refs[GP:2 * GP]
    knt_ref, vn_ref, o_ref, m_ref, l_ref, acc_ref = refs[2 * GP:]
    p = pl.program_id(1)
    n_heads = D_MODEL // LANES
    hrows = 2 * ROWS

    @pl.when(p == 0)
    def _():
        _init_softmax(m_ref, l_ref, acc_ref)

    def accumulate(s, values):
        alpha, pr = _softmax_update(s, m_ref, l_ref)
        shape = (n_heads * hrows, PAGE * n_heads)
        own = lax.broadcasted_iota(I32, shape, 0) // hrows == lax.broadcasted_iota(I32, shape, 1) % n_heads
        pv = 0.0
        for g, v in enumerate(values):
            spread = _dot(pr[:, g * PAGE:(g + 1) * PAGE].astype(BF16), expand_ref[...])
            pv = pv + _dot(jnp.where(own, spread, 0.0).astype(BF16), v)
        acc_ref[...] = alpha * acc_ref[...] + pv

    @pl.when(p < n_steps)
    def _():
        q = qbd_ref[...]
        s = jnp.concatenate([_dot(q, k[...].reshape(D_MODEL, PAGE).astype(BF16)) for k in k_refs], axis=1)
        accumulate(s, [v[...].reshape(PAGE * n_heads, LANES).astype(BF16) for v in v_refs])

    @pl.when(p == n_steps)
    def _():
        s = _dot(qbd_ref[...], knt_ref[...])
        s = jnp.where(_new_token_mask(s.shape, n_new), s, NEG)
        accumulate(s, [vn_ref[...]])
        accn = acc_ref[...] / l_ref[...]
        lam = _diff_lambda(lam_ref[...], lam_init)
        for h in range(n_heads):
            o = accn[h * hrows:h * hrows + ROWS] - lam * accn[h * hrows + ROWS:(h + 1) * hrows]
            o_ref[:, h * LANES:(h + 1) * LANES] = _rms(o, sg_ref[...]) * (1.0 - lam_init)


def _diff_sample(qbd, cache_kt, cache_v, knt, vn, pt, layer, lam_p, subln, lam_init, n_new):
    nb, n_rows, _ = qbd.shape
    n_pages = pt.shape[0] // nb
    assert n_pages % GP == 0
    n_steps = n_pages // GP
    per_b3 = lambda b, p, pt: (b, 0, 0)
    fixed = lambda b, p, pt: (0, 0)
    n_heads = D_MODEL // LANES
    expand = (lax.broadcasted_iota(I32, (PAGE, PAGE * n_heads), 0)
              == lax.broadcasted_iota(I32, (PAGE, PAGE * n_heads), 1) // n_heads).astype(BF16)
    grid_spec = pltpu.PrefetchScalarGridSpec(
        num_scalar_prefetch=1,
        grid=(nb, n_steps + 1),
        in_specs=[pl.BlockSpec((4, HEAD_DIM), fixed),
                  pl.BlockSpec((1, LANES), fixed),
                  pl.BlockSpec((None, n_rows, D_MODEL), per_b3),
                  pl.BlockSpec((PAGE, PAGE * n_heads), fixed)]
                 + _page_specs(cache_kt.shape[2:], GP, layer, n_pages)
                 + _page_specs(cache_v.shape[2:], GP, layer, n_pages)
                 + [pl.BlockSpec((None, D_MODEL, PAGE), per_b3),
                    pl.BlockSpec((None, PAGE * n_heads, LANES), per_b3)],
        out_specs=pl.BlockSpec((None, ROWS, D_MODEL), per_b3),
        scratch_shapes=[pltpu.VMEM((n_rows, 1), F32), pltpu.VMEM((n_rows, 1), F32),
                        pltpu.VMEM((n_rows, LANES), F32)])
    return pl.pallas_call(
        functools.partial(_diff_sample_kernel, lam_init=lam_init, n_steps=n_steps, n_new=n_new),
        grid_spec=grid_spec,
        out_shape=jax.ShapeDtypeStruct((nb, ROWS, D_MODEL), F32),
        compiler_params=_cparams(("arbitrary", "arbitrary")),
        name="diff_sample",
    )(pt, lam_p, subln, qbd, expand, *([cache_kt] * GP), *([cache_v] * GP), knt, vn)


QB = 128
CK = 512


def _dsa_prompt_kernel(qi_ref, wi_ref, ki_ref, q_ref, k_ref, vt_ref, o_ref, key_ref, bias_ref,
                       m_ref, l_ref, acc_ref, *, seq):
    i = pl.program_id(1)
    nch = ((i + 1) * QB + CK - 1) // CK
    kpos0 = lax.broadcasted_iota(I32, (CK, QB), 0)
    qpos = i * QB + lax.broadcasted_iota(I32, (CK, QB), 1)

    def index_chunk(c, carry):
        kic = ki_ref[pl.ds(pl.multiple_of(c * CK, CK), CK), :]
        sc = jnp.zeros((CK, QB), F32)
        for h in range(N_HEADS_IDX):
            sc = sc + wi_ref[h:h + 1, :] * jnp.maximum(_dot_nt(kic, qi_ref[h]), 0.0)
        key_ref[c] = _sortable(jnp.where(c * CK + kpos0 <= qpos, sc, -jnp.inf))
        return carry

    lax.fori_loop(0, nch, index_chunk, 0)

    def count(pred):
        def body(c, acc):
            hit = pred(key_ref[c], c * CK + kpos0).astype(I32)
            return acc + hit.reshape(CK // 8, 8, QB).sum(axis=0)
        acc = lax.fori_loop(0, nch, body, jnp.zeros((8, QB), I32))
        return jnp.sum(acc, axis=0, keepdims=True)

    thr = _kth_largest(lambda t: count(lambda key, kpos: key >= t), (1, QB))
    need = TOPK - count(lambda key, kpos: key > thr)
    n_tie = count(lambda key, kpos: key == thr)
    all_ties = jnp.max(jnp.where(n_tie > need, 1, 0)) == 0
    nbits = (seq - 1).bit_length() + 1
    cut = lax.cond(
        all_ties,
        lambda: jnp.full((1, QB), 2 ** nbits - 1, I32),
        lambda: _tie_cutoff(lambda j: count(lambda key, kpos: (key == thr) & (kpos < j)), need, (1, QB), nbits))

    def bias_chunk(c, carry):
        key = key_ref[c]
        kpos = c * CK + kpos0
        sel = ((key > thr) | ((key == thr) & (kpos < cut))) & (kpos <= qpos)
        bias_ref[c] = jnp.where(sel, 0.0, NEG)
        return carry

    lax.fori_loop(0, nch, bias_chunk, 0)

    _init_softmax(m_ref, l_ref, acc_ref)
    lo = _lane_iota((1, LANES)) < HEAD_DIM
    n_pairs = D_MODEL // LANES

    def attend_chunk(c, carry):
        bias = bias_ref[c]
        off = pl.multiple_of(c * CK, CK)
        for hp in range(n_pairs):
            sl = slice(hp * LANES, (hp + 1) * LANES)
            q = q_ref[:, sl]
            k = k_ref[pl.ds(off, CK), sl]
            vt = vt_ref[c, sl, :]
            for j in range(2):
                h = 2 * hp + j
                qh = jnp.where(lo, q, jnp.zeros_like(q)) if j == 0 else jnp.where(lo, jnp.zeros_like(q), q)
                s = _dot_nt(k, qh) + bias
                m_old = m_ref[h:h + 1, :]
                m_new = jnp.maximum(m_old, jnp.max(s, axis=0, keepdims=True))
                alpha = jnp.exp2(m_old - m_new)
                pr = jnp.exp2(s - m_new)
                l_ref[h:h + 1, :] = alpha * l_ref[h:h + 1, :] + jnp.sum(pr, axis=0, keepdims=True)
                m_ref[h:h + 1, :] = m_new
                rows = slice(h * HEAD_DIM, (h + 1) * HEAD_DIM)
                acc_ref[rows, :] = alpha * acc_ref[rows, :] + _dot(vt[j * HEAD_DIM:(j + 1) * HEAD_DIM],
                                                                   pr.astype(BF16))
        return carry

    lax.fori_loop(0, nch, attend_chunk, 0)
    for hp in range(n_pairs):
        ot = jnp.concatenate(
            [acc_ref[h * HEAD_DIM:(h + 1) * HEAD_DIM, :] / l_ref[h:h + 1, :] for h in (2 * hp, 2 * hp + 1)], axis=0)
        o_ref[:, hp * LANES:(hp + 1) * LANES] = ot.T.astype(BF16)


def _dsa_prompt(qi, wi, ki, q, k, vt, batch):
    m = q.shape[0]
    t = m // batch
    nq = t // QB
    assert t % CK == 0 and CK % QB == 0
    blk = lambda b, i: (b * nq + i, 0)
    per_b = lambda b, i: (b, 0)
    return pl.pallas_call(
        functools.partial(_dsa_prompt_kernel, seq=t),
        grid=(batch, nq),
        in_specs=[pl.BlockSpec((N_HEADS_IDX, QB, HEAD_DIM), lambda b, i: (0, b * nq + i, 0)),
                  pl.BlockSpec((N_HEADS_IDX, QB), lambda b, i: (0, b * nq + i)),
                  pl.BlockSpec((t, HEAD_DIM), per_b),
                  pl.BlockSpec((QB, D_MODEL), blk),
                  pl.BlockSpec((t, D_MODEL), per_b),
                  pl.BlockSpec((None, t // CK, D_MODEL, CK), lambda b, i: (b, 0, 0, 0))],
        out_specs=pl.BlockSpec((QB, D_MODEL), blk),
        out_shape=jax.ShapeDtypeStruct((m, D_MODEL), BF16),
        scratch_shapes=[pltpu.VMEM((t // CK, CK, QB), I32), pltpu.VMEM((t // CK, CK, QB), F32),
                        pltpu.VMEM((D_MODEL // HEAD_DIM, QB), F32), pltpu.VMEM((D_MODEL // HEAD_DIM, QB), F32),
                        pltpu.VMEM((D_MODEL, QB), F32)],
        compiler_params=_cparams(("arbitrary", "arbitrary")),
        name="dsa_prompt",
    )(qi, wi, ki, q, k, vt)


def _dsa_index_sample_kernel(pt_ref, qi_ref, wi_ref, *refs, n_pages, n_new):
    k_refs = refs[:GI]
    kn_ref, key_out, thr_out, cut_out, key_ref = refs[GI:]
    p = pl.program_id(1)
    n_steps = n_pages // GI

    def page_keys(kt, mask=None):
        s = jnp.maximum(_dot(qi_ref[...], kt), 0.0) * wi_ref[...]
        sc = s.reshape(N_HEADS_IDX, ROWS, PAGE).sum(axis=0)
        if mask is not None:
            sc = jnp.where(mask, sc, -jnp.inf)
        return _sortable(sc)

    @pl.when(p < n_steps)
    def _():
        for g, k in enumerate(k_refs):
            key = page_keys(k[...].astype(BF16))
            key_ref[p * GI + g] = key
            key_out[g] = key

    @pl.when(p == n_steps)
    def _():
        key = page_keys(kn_ref[...], _new_token_mask((ROWS, PAGE), n_new))
        key_ref[n_pages] = key
        key_out[0] = key
        key_out[1:] = jnp.full((GI - 1, ROWS, PAGE), INT_MIN, I32)
        keys = key_ref[...]
        idx = (lax.broadcasted_iota(I32, keys.shape, 0) * PAGE + lax.broadcasted_iota(I32, keys.shape, 2))

        def count(pred):
            return jnp.sum(jnp.sum(pred.astype(I32), axis=0), axis=-1, keepdims=True)

        thr = _kth_largest(lambda t: count(keys >= t[None]), (ROWS, 1))
        need = TOPK - count(keys > thr[None])
        nbits = ((n_pages + 1) * PAGE - 1).bit_length() + 1
        cut = _tie_cutoff(lambda j: count((keys == thr[None]) & (idx < j[None])), need, (ROWS, 1), nbits)
        thr_out[...] = jnp.broadcast_to(thr, (ROWS, LANES))
        cut_out[...] = jnp.broadcast_to(cut, (ROWS, LANES))


def _dsa_index_sample(qi, wi, cache_kit, kin_t, pt, layer, n_new):
    nb = qi.shape[0]
    n_pages = pt.shape[0] // nb
    assert n_pages % GI == 0
    n_steps = n_pages // GI
    per_b = lambda b, p, pt: (b, 0, 0)
    grid_spec = pltpu.PrefetchScalarGridSpec(
        num_scalar_prefetch=1,
        grid=(nb, n_steps + 1),
        in_specs=[pl.BlockSpec((None, N_HEADS_IDX * ROWS, HEAD_DIM), per_b),
                  pl.BlockSpec((None, N_HEADS_IDX * ROWS, PAGE), per_b)]
                 + _page_specs(cache_kit.shape[2:], GI, layer, n_pages)
                 + [pl.BlockSpec((None, HEAD_DIM, PAGE), per_b)],
        out_specs=[pl.BlockSpec((None, GI, ROWS, PAGE), lambda b, p, pt: (b, p, 0, 0)),
                   pl.BlockSpec((None, ROWS, LANES), per_b),
                   pl.BlockSpec((None, ROWS, LANES), per_b)],
        scratch_shapes=[pltpu.VMEM((n_pages + 1, ROWS, PAGE), I32)])
    return pl.pallas_call(
        functools.partial(_dsa_index_sample_kernel, n_pages=n_pages, n_new=n_new),
        grid_spec=grid_spec,
        out_shape=[jax.ShapeDtypeStruct((nb, n_pages + GI, ROWS, PAGE), I32),
                   jax.ShapeDtypeStruct((nb, ROWS, LANES), I32),
                   jax.ShapeDtypeStruct((nb, ROWS, LANES), I32)],
        compiler_params=_cparams(("arbitrary", "arbitrary")),
        name="dsa_index_sample",
    )(pt, qi, wi, *([cache_kit] * GI), kin_t)


def _dsa_sample_kernel(pt_ref, qbd_ref, key_in, thr_ref, cut_ref, *refs, n_steps, n_new):
    k_refs, v_refs = refs[:GP], refs[GP:2 * GP]
    knt_ref, vnt_ref, o_ref, m_ref, l_ref, acc_ref = refs[2 * GP:]
    p = pl.program_id(1)
    n_heads = D_MODEL // HEAD_DIM

    @pl.when(p == 0)
    def _():
        _init_softmax(m_ref, l_ref, acc_ref)

    def selected(g, extra=None):
        key = key_in[g]
        idx = (p * GP + g) * PAGE + lax.broadcasted_iota(I32, key.shape, 1)
        thr = thr_ref[...]
        sel = (key > thr) | ((key == thr) & (idx < cut_ref[...]))
        if extra is not None:
            sel = sel & extra
        return jnp.where(sel, 1, 0)

    def accumulate(s, sel, head_values):
        sel = jnp.concatenate([sel] * n_heads, axis=0) > 0
        alpha, pr = _softmax_update(jnp.where(sel, s, NEG), m_ref, l_ref)
        pv = []
        for h in range(n_heads):
            ph = pr[h * ROWS:(h + 1) * ROWS].astype(BF16)
            pv.append(sum(_dot_nt(ph[:, g * PAGE:(g + 1) * PAGE], vt) for g, vt in enumerate(head_values(h))))
        acc_ref[...] = alpha * acc_ref[...] + jnp.concatenate(pv, axis=0)

    @pl.when(p < n_steps)
    def _():
        q = qbd_ref[...]
        s = jnp.concatenate([_dot(q, k[...].reshape(D_MODEL, PAGE).astype(BF16)) for k in k_refs], axis=1)
        sel = jnp.concatenate([selected(g) for g in range(GP)], axis=1)
        accumulate(s, sel, lambda h: [v[h].astype(BF16) for v in v_refs])

    @pl.when(p == n_steps)
    def _():
        s = _dot(qbd_ref[...], knt_ref[...])
        accumulate(s, selected(0, _new_token_mask((ROWS, PAGE), n_new)), lambda h: [vnt_ref[h]])
        o_ref[...] = acc_ref[...] / l_ref[...]


def _dsa_sample(qbd, keys, thr, cut, cache_kt, cache_vt, knt, vnt, pt, layer, n_new):
    nb, n_rows, _ = qbd.shape
    n_pages = pt.shape[0] // nb
    assert n_pages % GP == 0
    n_steps = n_pages // GP
    per_b3 = lambda b, p, pt: (b, 0, 0)
    per_b4 = lambda b, p, pt: (b, 0, 0, 0)
    grid_spec = pltpu.PrefetchScalarGridSpec(
        num_scalar_prefetch=1,
        grid=(nb, n_steps + 1),
        in_specs=[pl.BlockSpec((None, n_rows, D_MODEL), per_b3),
                  pl.BlockSpec((None, GP, ROWS, PAGE), lambda b, p, pt: (b, p, 0, 0)),
                  pl.BlockSpec((None, ROWS, LANES), per_b3),
                  pl.BlockSpec((None, ROWS, LANES), per_b3)]
                 + _page_specs(cache_kt.shape[2:], GP, layer, n_pages)
                 + _page_specs(cache_vt.shape[2:], GP, layer, n_pages)
                 + [pl.BlockSpec((None, D_MODEL, PAGE), per_b3),
                    pl.BlockSpec((None,) + tuple(vnt.shape[1:]), per_b4)],
        out_specs=pl.BlockSpec((None, n_rows, HEAD_DIM), per_b3),
        scratch_shapes=[pltpu.VMEM((n_rows, 1), F32), pltpu.VMEM((n_rows, 1), F32),
                        pltpu.VMEM((n_rows, HEAD_DIM), F32)])
    return pl.pallas_call(
        functools.partial(_dsa_sample_kernel, n_steps=n_steps, n_new=n_new),
        grid_spec=grid_spec,
        out_shape=jax.ShapeDtypeStruct((nb, n_rows, HEAD_DIM), F32),
        compiler_params=_cparams(("arbitrary", "arbitrary")),
        name="dsa_sample",
    )(pt, qbd, keys, thr, cut, *([cache_kt] * GP), *([cache_vt] * GP), knt, vnt)


def _tile_gain(g):
    return jnp.tile(g.astype(F32), LANES // HEAD_DIM)[None, :]


def _pad_last(a, n):
    return jnp.pad(a, [(0, 0)] * (a.ndim - 1) + [(0, n - a.shape[-1])])


def kernel(x_prompt, x_sample, cache_a_k, cache_a_v, cache_b_k, cache_b_v, cache_b_kidx, page_table,
           p_prompt, p_sample, norm_mix, norm_ffn, norm_ple, a_w_qkv, a_w_o, a_q_norm, a_k_norm, a_lambda,
           a_subln, b_w_qkv, b_w_o, b_q_norm, b_k_norm, b_kidx_norm, w_ffn_in, w_ffn_out, w_ple_gate,
           w_ple_proj):
    batch, seq, _ = x_prompt.shape
    nb, n_new, _ = x_sample.shape
    depth = norm_mix.shape[0]
    page_size = cache_a_k.shape[2]
    n_pages = page_table.shape[1]
    past = n_pages * page_size
    mp, ms = batch * seq, nb * n_new
    tm_p = min(256, mp)
    n_sub = D_MODEL // HEAD_DIM

    xp = x_prompt.reshape(mp, D_MODEL)
    xs = x_sample.reshape(ms, D_MODEL)
    pt = page_table.reshape(-1).astype(I32)
    cos_p, sin_p = _rope_tables(jnp.arange(seq, dtype=I32))
    cos_s, sin_s = _rope_tables(jnp.tile(past + jnp.arange(n_new, dtype=I32), nb))
    ca_kt = cache_a_k.transpose(0, 1, 3, 4, 2)
    cb_kt = cache_b_k.transpose(0, 1, 3, 4, 2)
    cb_vt = cache_b_v.transpose(0, 1, 3, 4, 2)
    cb_kit = cache_b_kidx.transpose(0, 1, 3, 2)
    row_vec = lambda g: g.astype(F32)[None, :]

    def tokens8(a):
        a = a.reshape(nb, n_new, a.shape[-1])
        return jnp.pad(a, ((0, 0), (0, ROWS - n_new), (0, 0)))

    def keys_t(a):
        return _pad_last(a.reshape(nb, n_new, a.shape[-1]).transpose(0, 2, 1), PAGE)

    sub = jnp.arange(n_sub)
    own_a = (sub[None, None, :] == 2 * jnp.arange(n_sub // 2)[:, None, None] + jnp.arange(2)[None, :, None])
    own_b = sub[None, :] == sub[:, None]

    def qbd_a(q):
        q8 = tokens8(q).reshape(nb, 1, 1, ROWS, n_sub, HEAD_DIM)
        return jnp.where(own_a[None, :, :, None, :, None], q8, 0).reshape(nb, 2 * n_sub // 2 * ROWS, D_MODEL)

    def qbd_b(q):
        q8 = tokens8(q).reshape(nb, 1, ROWS, n_sub, HEAD_DIM)
        return jnp.where(own_b[None, :, None, :, None], q8, 0).reshape(nb, n_sub * ROWS, D_MODEL)

    outs = {name: [] for name in ("akp", "avp", "bkp", "bvp", "bkip", "aks", "avs", "bks", "bvs", "bkis")}
    for i in range(depth):
        j = i // 2
        gn = row_vec(norm_mix[i])
        if i % 2 == 0:
            lam_init = 0.8 - 0.6 * math.exp(-0.3 * i)
            w = a_w_qkv[j].astype(BF16)
            qg, kg = _tile_gain(a_q_norm[j]), _tile_gain(a_k_norm[j])
            lam_p, subln = a_lambda[j].astype(F32), row_vec(a_subln[j])
            wo = a_w_o[j].astype(BF16)
            q, k32, k16, v32, v16 = _proj_a(xp, gn, w, qg, kg, cos_p, sin_p, tm_p)
            op = _diff_prompt(q, k16, v16, lam_p, subln, lam_init, batch, min(512, seq), min(512, seq))
            outs["akp"].append(k32)
            outs["avp"].append(v32)
            q, k32, k16, v32, v16 = _proj_a(xs, gn, w, qg, kg, cos_s, sin_s, ms)
            vn = jnp.pad(v16.reshape(nb, n_new, D_MODEL), ((0, 0), (0, PAGE - n_new), (0, 0)))
            vn = vn.reshape(nb, PAGE * n_sub // 2, LANES)
            os_ = _diff_sample(qbd_a(q), ca_kt, cache_a_v, keys_t(k16), vn, pt, j, lam_p, subln, lam_init, n_new)
            os_ = os_[:, :n_new].reshape(ms, D_MODEL).astype(BF16)
            outs["aks"].append(k32)
            outs["avs"].append(v32)
        else:
            w = b_w_qkv[j].astype(BF16)
            wt = _pad_last(w[:, B_TAIL:], LANES)
            w = w[:, :B_TAIL]
            qg, kg, kig = _tile_gain(b_q_norm[j]), _tile_gain(b_k_norm[j]), _tile_gain(b_kidx_norm[j])
            wo = b_w_o[j].astype(BF16)
            q, k32, k16, v32, v16, qi, tail, ki16 = _proj_b(xp, gn, w, wt, qg, kg, kig, cos_p, sin_p, tm_p)
            wi_t = tail[:, HEAD_DIM:HEAD_DIM + N_HEADS_IDX].T
            vt = v16.reshape(batch, seq // CK, CK, D_MODEL).transpose(0, 1, 3, 2)
            op = _dsa_prompt(qi, wi_t, ki16, q, k16, vt, batch)
            outs["bkp"].append(k32)
            outs["bvp"].append(v32)
            outs["bkip"].append(tail[:, :HEAD_DIM])
            q, k32, k16, v32, v16, qi, tail, ki16 = _proj_b(xs, gn, w, wt, qg, kg, kig, cos_s, sin_s, ms)
            qi8 = jnp.pad(qi.reshape(N_HEADS_IDX, nb, n_new, HEAD_DIM), ((0, 0), (0, 0), (0, ROWS - n_new), (0, 0)))
            qi8 = qi8.transpose(1, 0, 2, 3).reshape(nb, N_HEADS_IDX * ROWS, HEAD_DIM)
            wi8 = tokens8(tail[:, HEAD_DIM:HEAD_DIM + N_HEADS_IDX]).transpose(0, 2, 1)
            wi8 = jnp.broadcast_to(wi8.reshape(nb, N_HEADS_IDX * ROWS, 1), (nb, N_HEADS_IDX * ROWS, PAGE))
            keys, thr, cut = _dsa_index_sample(qi8, wi8, cb_kit, keys_t(ki16), pt, j, n_new)
            vnt = _pad_last(v16.reshape(nb, n_new, n_sub, HEAD_DIM).transpose(0, 2, 3, 1), PAGE)
            os_ = _dsa_sample(qbd_b(q), keys, thr, cut, cb_kt, cb_vt, keys_t(k16), vnt, pt, j, n_new)
            os_ = os_.reshape(nb, n_sub, ROWS, HEAD_DIM)[:, :, :n_new].transpose(0, 2, 1, 3)
            os_ = os_.reshape(ms, D_MODEL).astype(BF16)
            outs["bks"].append(k32)
            outs["bvs"].append(v32)
            outs["bkis"].append(tail[:, :HEAD_DIM])
        tail_w = (wo, row_vec(norm_ffn[i]), w_ffn_in[i].astype(BF16), w_ffn_out[i].astype(BF16),
                  row_vec(norm_ple[i]), w_ple_gate[i].astype(BF16), w_ple_proj[i].astype(BF16))
        xp = _tail(xp, op, p_prompt[i].reshape(mp, D_PLE), *tail_w, tm=tm_p)
        xs = _tail(xs, os_, p_sample[i].reshape(ms, D_PLE), *tail_w, tm=ms)

    n_a, n_b = D_MODEL // (2 * HEAD_DIM), D_MODEL // HEAD_DIM
    st = lambda name, shape: jnp.stack(outs[name]).reshape((len(outs[name]),) + shape)
    return (xp.reshape(batch, seq, D_MODEL), xs.reshape(nb, n_new, D_MODEL),
            st("akp", (batch, seq, 2 * n_a, HEAD_DIM)), st("avp", (batch, seq, n_a, 2 * HEAD_DIM)),
            st("bkp", (batch, seq, n_b, HEAD_DIM)), st("bvp", (batch, seq, n_b, HEAD_DIM)),
            st("bkip", (batch, seq, HEAD_DIM)),
            st("aks", (nb, n_new, 2 * n_a, HEAD_DIM)), st("avs", (nb, n_new, n_a, 2 * HEAD_DIM)),
            st("bks", (nb, n_new, n_b, HEAD_DIM)), st("bvs", (nb, n_new, n_b, HEAD_DIM)),
            st("bkis", (nb, n_new, HEAD_DIM)))
```

```python
import functools
import math

import jax
import jax.numpy as jnp
from jax import lax
from jax.experimental import pallas as pl
from jax.experimental.pallas import tpu as pltpu

F32 = jnp.float32
BF16 = jnp.bfloat16
I32 = jnp.int32

D_MODEL = 1024
HEAD_DIM = 64
LANES = 128
N_HEADS_IDX = 8
TOPK = 256
ROPE_THETA = 10000.0
RMS_EPS = 1e-6
D_FF = 2816
D_PLE = 256
PAGE = 128
NEG = -1e30
INT_MIN = -(2 ** 31)
VMEM_LIMIT = 56 * 1024 * 1024
Q_SCALE = HEAD_DIM ** -0.5 * math.log2(math.e)

_NT = (((1,), (1,)), ((), ()))


def _dot(a, b):
    return jnp.dot(a, b, preferred_element_type=F32)


def _dot_nt(a, b):
    return lax.dot_general(a, b, _NT, preferred_element_type=F32)


def _cparams(sem):
    return pltpu.CompilerParams(dimension_semantics=sem, vmem_limit_bytes=VMEM_LIMIT)


def _rms(x, g):
    return x * lax.rsqrt(jnp.mean(x * x, axis=-1, keepdims=True) + RMS_EPS) * g


def _lane_iota(shape):
    return lax.broadcasted_iota(I32, shape, len(shape) - 1)


def _head_norm(y, gmat, g):
    ss = _dot((y * y).astype(BF16), gmat)
    return y * lax.rsqrt(ss * (1.0 / HEAD_DIM) + RMS_EPS) * g


def _rope(y, cos, sin, first_half):
    rot = jnp.where(first_half, pltpu.roll(y, LANES - HEAD_DIM // 2, 1), pltpu.roll(y, HEAD_DIM // 2, 1))
    return y * cos + rot * sin


def _rope_tables(pos):
    half = HEAD_DIM // 2
    inv_freq = ROPE_THETA ** (-jnp.arange(half, dtype=F32) * 2.0 / HEAD_DIM)
    ang = pos.astype(F32)[:, None] * inv_freq[None, :]
    cos, sin = jnp.cos(ang), jnp.sin(ang)
    return (jnp.concatenate([cos, cos, cos, cos], axis=-1),
            jnp.concatenate([-sin, sin, -sin, sin], axis=-1))


def _group_matrix():
    r = lax.broadcasted_iota(I32, (LANES, LANES), 0) // HEAD_DIM
    c = lax.broadcasted_iota(I32, (LANES, LANES), 1) // HEAD_DIM
    return (r == c).astype(BF16)


def _sortable(score):
    score = jnp.where(score == 0.0, 0.0, score)
    bits = lax.bitcast_convert_type(score, I32)
    return bits ^ ((bits >> 31) & 0x7FFFFFFF)


def _kth_largest(count_ge, shape):
    t0 = jnp.where(count_ge(jnp.zeros(shape, I32)) >= TOPK, 0, INT_MIN).astype(I32)

    def body(b, t):
        cand = t | (1 << (30 - b))
        return jnp.where(count_ge(cand) >= TOPK, cand, t)

    return lax.fori_loop(0, 31, body, t0)


def _tie_cutoff(count_tie_lt, need, shape, nbits):
    def body(b, j):
        cand = j | (1 << (nbits - 1 - b))
        return jnp.where(count_tie_lt(cand) <= need, cand, j)

    return lax.fori_loop(0, nbits, body, jnp.zeros(shape, I32))


def _qkv_heads(h, w_ref, qg_ref, kg_ref, cos, sin, gmat, first, q_ref, k32_ref, k16_ref, v32_ref, v16_ref):
    yq = _dot(h, w_ref[:, 0:D_MODEL])
    yk = _dot(h, w_ref[:, D_MODEL:2 * D_MODEL])
    for c in range(D_MODEL // LANES):
        sl = slice(c * LANES, (c + 1) * LANES)
        q = _rope(_head_norm(yq[:, sl], gmat, qg_ref[...]), cos, sin, first)
        q_ref[:, sl] = (q * Q_SCALE).astype(BF16)
        k = _rope(_head_norm(yk[:, sl], gmat, kg_ref[...]), cos, sin, first)
        k32_ref[:, sl] = k
        k16_ref[:, sl] = k.astype(BF16)
    v = _dot(h, w_ref[:, 2 * D_MODEL:3 * D_MODEL])
    v32_ref[...] = v
    v16_ref[...] = v.astype(BF16)


def _proj_a_kernel(x_ref, gn_ref, w_ref, qg_ref, kg_ref, cos_ref, sin_ref, gmat_ref,
                   q_ref, k32_ref, k16_ref, v32_ref, v16_ref):
    h = _rms(x_ref[...], gn_ref[...]).astype(BF16)
    first = (_lane_iota((1, LANES)) % HEAD_DIM) < HEAD_DIM // 2
    _qkv_heads(h, w_ref, qg_ref, kg_ref, cos_ref[...], sin_ref[...], gmat_ref[...], first,
               q_ref, k32_ref, k16_ref, v32_ref, v16_ref)


def _proj_a(x, gn, w, qg, kg, cos, sin, tm):
    m = x.shape[0]
    nt = cos.shape[0] // tm
    row = lambda i: (i, 0)
    fixed = lambda i: (0, 0)
    wide = pl.BlockSpec((tm, D_MODEL), row)
    return pl.pallas_call(
        _proj_a_kernel,
        grid=(m // tm,),
        in_specs=[wide,
                  pl.BlockSpec((1, D_MODEL), fixed),
                  pl.BlockSpec((D_MODEL, 3 * D_MODEL), fixed),
                  pl.BlockSpec((1, LANES), fixed),
                  pl.BlockSpec((1, LANES), fixed),
                  pl.BlockSpec((tm, LANES), lambda i: (i % nt, 0)),
                  pl.BlockSpec((tm, LANES), lambda i: (i % nt, 0)),
                  pl.BlockSpec((LANES, LANES), fixed)],
        out_specs=[wide, wide, wide, wide, wide],
        out_shape=[jax.ShapeDtypeStruct((m, D_MODEL), BF16),
                   jax.ShapeDtypeStruct((m, D_MODEL), F32),
                   jax.ShapeDtypeStruct((m, D_MODEL), BF16),
                   jax.ShapeDtypeStruct((m, D_MODEL), F32),
                   jax.ShapeDtypeStruct((m, D_MODEL), BF16)],
        compiler_params=_cparams(("arbitrary",)),
        name="proj_a",
    )(x, gn, w, qg, kg, cos, sin, _group_matrix())


B_QI = 3 * D_MODEL
B_TAIL = B_QI + N_HEADS_IDX * HEAD_DIM


def _proj_b_kernel(x_ref, gn_ref, w_ref, wt_ref, qg_ref, kg_ref, kig_ref, cos_ref, sin_ref, gmat_ref,
                   q_ref, k32_ref, k16_ref, v32_ref, v16_ref, qi_ref, tail_ref, ki16_ref):
    h = _rms(x_ref[...], gn_ref[...]).astype(BF16)
    cos, sin = cos_ref[...], sin_ref[...]
    lane = _lane_iota((1, LANES))
    first = (lane % HEAD_DIM) < HEAD_DIM // 2
    _qkv_heads(h, w_ref, qg_ref, kg_ref, cos, sin, gmat_ref[...], first,
               q_ref, k32_ref, k16_ref, v32_ref, v16_ref)
    yi = _dot(h, w_ref[:, B_QI:B_TAIL])
    for c in range(N_HEADS_IDX * HEAD_DIM // LANES):
        qi = _rope(yi[:, c * LANES:(c + 1) * LANES], cos, sin, first) * HEAD_DIM ** -0.5
        qi_ref[2 * c] = qi[:, :HEAD_DIM].astype(BF16)
        qi_ref[2 * c + 1] = qi[:, HEAD_DIM:].astype(BF16)
    yt = _dot(h, wt_ref[...])
    is_key = lane < HEAD_DIM
    ms = jnp.sum(jnp.where(is_key, yt * yt, 0.0), axis=-1, keepdims=True) * (1.0 / HEAD_DIM)
    ki = _rope(yt * lax.rsqrt(ms + RMS_EPS) * kig_ref[...], cos, sin, first)
    tail = jnp.where(is_key, ki, yt * N_HEADS_IDX ** -0.5)
    tail_ref[...] = tail
    ki16_ref[...] = tail[:, :HEAD_DIM].astype(BF16)


def _proj_b(x, gn, w, wt, qg, kg, kig, cos, sin, tm):
    m = x.shape[0]
    nt = cos.shape[0] // tm
    row = lambda i: (i, 0)
    fixed = lambda i: (0, 0)
    wide = pl.BlockSpec((tm, D_MODEL), row)
    return pl.pallas_call(
        _proj_b_kernel,
        grid=(m // tm,),
        in_specs=[wide,
                  pl.BlockSpec((1, D_MODEL), fixed),
                  pl.BlockSpec((D_MODEL, B_TAIL), fixed),
                  pl.BlockSpec((D_MODEL, LANES), fixed),
                  pl.BlockSpec((1, LANES), fixed),
                  pl.BlockSpec((1, LANES), fixed),
                  pl.BlockSpec((1, LANES), fixed),
                  pl.BlockSpec((tm, LANES), lambda i: (i % nt, 0)),
                  pl.BlockSpec((tm, LANES), lambda i: (i % nt, 0)),
                  pl.BlockSpec((LANES, LANES), fixed)],
        out_specs=[wide, wide, wide, wide, wide,
                   pl.BlockSpec((N_HEADS_IDX, tm, HEAD_DIM), lambda i: (0, i, 0)),
                   pl.BlockSpec((tm, LANES), row),
                   pl.BlockSpec((tm, HEAD_DIM), row)],
        out_shape=[jax.ShapeDtypeStruct((m, D_MODEL), BF16),
                   jax.ShapeDtypeStruct((m, D_MODEL), F32),
                   jax.ShapeDtypeStruct((m, D_MODEL), BF16),
                   jax.ShapeDtypeStruct((m, D_MODEL), F32),
                   jax.ShapeDtypeStruct((m, D_MODEL), BF16),
                   jax.ShapeDtypeStruct((N_HEADS_IDX, m, HEAD_DIM), BF16),
                   jax.ShapeDtypeStruct((m, LANES), F32),
                   jax.ShapeDtypeStruct((m, HEAD_DIM), BF16)],
        compiler_params=_cparams(("arbitrary",)),
        name="proj_b",
    )(x, gn, w, wt, qg, kg, kig, cos, sin, _group_matrix())


FF_CHUNK = 256


def _tail_kernel(x_ref, o_ref, p_ref, wo_ref, gf_ref, win_ref, wout_ref, gp_ref, wg_ref, wp_ref,
                 y_ref, acc_ref):
    x1 = x_ref[...] + _dot(o_ref[...], wo_ref[...])
    h = _rms(x1, gf_ref[...]).astype(BF16)
    for c in range(D_FF // FF_CHUNK):
        g = _dot(h, win_ref[:, c * FF_CHUNK:(c + 1) * FF_CHUNK])
        u = _dot(h, win_ref[:, D_FF + c * FF_CHUNK:D_FF + (c + 1) * FF_CHUNK])
        a = (g * jax.nn.sigmoid(g) * u).astype(BF16)
        d = _dot(a, wout_ref[c * FF_CHUNK:(c + 1) * FF_CHUNK, :])
        if c == 0:
            acc_ref[...] = x1 + d
        else:
            acc_ref[...] += d
    x2 = acc_ref[...]
    gate = jax.nn.sigmoid(_dot(_rms(x2, gp_ref[...]).astype(BF16), wg_ref[...]))
    y_ref[...] = x2 + gate * _dot(p_ref[...].astype(BF16), wp_ref[...])


def _tail(x, o, p, wo, gf, win, wout, gp, wg, wp, tm):
    m = x.shape[0]
    row = lambda i: (i, 0)
    fixed = lambda i: (0, 0)
    once = dict(pipeline_mode=pl.Buffered(1))
    return pl.pallas_call(
        _tail_kernel,
        grid=(m // tm,),
        in_specs=[pl.BlockSpec((tm, D_MODEL), row),
                  pl.BlockSpec((tm, D_MODEL), row),
                  pl.BlockSpec((tm, D_PLE), row),
                  pl.BlockSpec((D_MODEL, D_MODEL), fixed, **once),
                  pl.BlockSpec((1, D_MODEL), fixed),
                  pl.BlockSpec((D_MODEL, 2 * D_FF), fixed, **once),
                  pl.BlockSpec((D_FF, D_MODEL), fixed, **once),
                  pl.BlockSpec((1, D_MODEL), fixed),
                  pl.BlockSpec((D_MODEL, D_MODEL), fixed, **once),
                  pl.BlockSpec((D_PLE, D_MODEL), fixed, **once)],
        out_specs=pl.BlockSpec((tm, D_MODEL), row),
        out_shape=jax.ShapeDtypeStruct((m, D_MODEL), F32),
        scratch_shapes=[pltpu.VMEM((tm, D_MODEL), F32)],
        compiler_params=_cparams(("arbitrary",)),
        name="tail",
    )(x, o, p, wo, gf, win, wout, gp, wg, wp)


def _diff_lambda(lp, lam_init):
    a = jnp.sum(lp[0:1] * lp[1:2], axis=-1, keepdims=True)
    b = jnp.sum(lp[2:3] * lp[3:4], axis=-1, keepdims=True)
    return jnp.exp(a) - jnp.exp(b) + lam_init


def _softmax_update(s, m_ref, l_ref):
    m_old = m_ref[...]
    m_new = jnp.maximum(m_old, jnp.max(s, axis=-1, keepdims=True))
    alpha = jnp.exp2(m_old - m_new)
    pr = jnp.exp2(s - m_new)
    l_ref[...] = alpha * l_ref[...] + jnp.sum(pr, axis=-1, keepdims=True)
    m_ref[...] = m_new
    return alpha, pr


def _init_softmax(m_ref, l_ref, acc_ref):
    m_ref[...] = jnp.full(m_ref.shape, NEG, F32)
    l_ref[...] = jnp.zeros(l_ref.shape, F32)
    acc_ref[...] = jnp.zeros(acc_ref.shape, F32)


def _diff_prompt_kernel(qt_ref, kt_ref, lam_ref, sg_ref, q_ref, k_ref, vt_ref, o_ref,
                        qs_ref, m_ref, l_ref, acc_ref, *, lam_init, tq, tk):
    step = pl.program_id(2)
    qi, ki = qt_ref[step], kt_ref[step]

    @pl.when(ki == 0)
    def _():
        _init_softmax(m_ref, l_ref, acc_ref)
        q = q_ref[...]
        lo = _lane_iota((1, LANES)) < HEAD_DIM
        qs_ref[...] = jnp.concatenate([jnp.where(lo, q, jnp.zeros_like(q)), jnp.where(lo, jnp.zeros_like(q), q)],
                                      axis=0)

    def scores(masked):
        k, vt = k_ref[...], vt_ref[...]
        sub = min(tq, 512)
        n_sub = 2 * tq // sub

        def sub_scores(g):
            return _dot_nt(k, qs_ref[g * sub:(g + 1) * sub, :])

        def accumulate(g, alpha, pv):
            cols = slice(g * sub, (g + 1) * sub)
            acc_ref[:, cols] = alpha * acc_ref[:, cols] + pv

        s_next = sub_scores(0)
        pending = None
        for g in range(n_sub):
            s = s_next
            if g + 1 < n_sub:
                s_next = sub_scores(g + 1)
            if masked:
                kpos = ki * tk + lax.broadcasted_iota(I32, (tk, sub), 0)
                qpos = qi * tq + (g * sub) % tq + lax.broadcasted_iota(I32, (tk, sub), 1)
                s = jnp.where(kpos <= qpos, s, NEG)
            cols = slice(g * sub, (g + 1) * sub)
            m_old = m_ref[:, cols]
            m_new = jnp.maximum(m_old, jnp.max(s, axis=0, keepdims=True))
            alpha = jnp.exp2(m_old - m_new)
            pr = jnp.exp2(s - m_new)
            l_ref[:, cols] = alpha * l_ref[:, cols] + jnp.sum(pr, axis=0, keepdims=True)
            m_ref[:, cols] = m_new
            pv = _dot(vt, pr.astype(BF16))
            if pending is not None:
                accumulate(*pending)
            pending = (g, alpha, pv)
        accumulate(*pending)

    straddles = ki * tk + tk - 1 > qi * tq
    pl.when(jnp.logical_not(straddles))(lambda: scores(False))
    pl.when(straddles)(lambda: scores(True))

    @pl.when((ki + 1) * tk >= (qi + 1) * tq)
    def _():
        lam = _diff_lambda(lam_ref[...], lam_init)
        a = acc_ref[...] / l_ref[...]
        o = a[:, :tq] - lam * a[:, tq:]
        o = o * lax.rsqrt(jnp.mean(o * o, axis=0, keepdims=True) + RMS_EPS) * sg_ref[...]
        o_ref[...] = (o * (1.0 - lam_init)).T.astype(BF16)


def _diff_prompt(q, k, vt, lam_p, subln_col, lam_init, batch, tq, tk):
    m = q.shape[0]
    t = m // batch
    nq = t // tq
    pairs = [(a, b) for a in range(nq) for b in range(((a + 1) * tq + tk - 1) // tk)]
    qt = jnp.array([a for a, _ in pairs], I32)
    kt = jnp.array([b for _, b in pairs], I32)
    n_heads = D_MODEL // LANES
    qmap = lambda b, h, s, qt, kt: (b * nq + qt[s], h)
    kmap = lambda b, h, s, qt, kt: (b * (t // tk) + kt[s], h)
    grid_spec = pltpu.PrefetchScalarGridSpec(
        num_scalar_prefetch=2,
        grid=(batch, n_heads, len(pairs)),
        in_specs=[pl.BlockSpec((4, HEAD_DIM), lambda b, h, s, qt, kt: (0, 0)),
                  pl.BlockSpec((LANES, 1), lambda b, h, s, qt, kt: (0, 0)),
                  pl.BlockSpec((tq, LANES), qmap),
                  pl.BlockSpec((tk, LANES), kmap),
                  pl.BlockSpec((LANES, tk), lambda b, h, s, qt, kt: (b * n_heads + h, kt[s]))],
        out_specs=pl.BlockSpec((tq, LANES), qmap),
        scratch_shapes=[pltpu.VMEM((2 * tq, LANES), BF16),
                        pltpu.VMEM((1, 2 * tq), F32), pltpu.VMEM((1, 2 * tq), F32),
                        pltpu.VMEM((LANES, 2 * tq), F32)])
    return pl.pallas_call(
        functools.partial(_diff_prompt_kernel, lam_init=lam_init, tq=tq, tk=tk),
        grid_spec=grid_spec,
        out_shape=jax.ShapeDtypeStruct((m, D_MODEL), BF16),
        compiler_params=_cparams(("arbitrary", "arbitrary", "arbitrary")),
        name="diff_prompt",
    )(qt, kt, lam_p, subln_col, q, k, vt)


ROWS = 8
GP = 4
GI = 16


def _page_map(g, group, layer, n_pages, ndim):
    def index_map(b, p, pt):
        page = pt[b * n_pages + jnp.minimum(p * group + g, n_pages - 1)]
        return (layer, page) + (0,) * ndim
    return index_map


def _page_specs(block, group, layer, n_pages):
    return [pl.BlockSpec((None, None) + tuple(block), _page_map(g, group, layer, n_pages, len(block)))
            for g in range(group)]


def _new_token_mask(shape, n_new):
    tok = lax.broadcasted_iota(I32, shape, 0) % ROWS
    col = lax.broadcasted_iota(I32, shape, 1)
    return (col <= tok) & (col < n_new)


def _diff_sample_kernel(pt_ref, lam_ref, sg_ref, qbd_ref, expand_ref, *refs, lam_init, n_steps, n_new):
    k_refs, v_refs = refs[:GP], refs[GP:2 * GP]
    knt_ref, vn_ref, o_ref, m_ref, l_ref, acc_ref = refs[2 * GP:]
    p = pl.program_id(1)
    n_heads = D_MODEL // LANES
    hrows = 2 * ROWS

    @pl.when(p == 0)
    def _():
        _init_softmax(m_ref, l_ref, acc_ref)

    def accumulate(s, values):
        alpha, pr = _softmax_update(s, m_ref, l_ref)
        shape = (n_heads * hrows, PAGE * n_heads)
        own = lax.broadcasted_iota(I32, shape, 0) // hrows == lax.broadcasted_iota(I32, shape, 1) % n_heads
        pv = 0.0
        for g, v in enumerate(values):
            spread = _dot(pr[:, g * PAGE:(g + 1) * PAGE].astype(BF16), expand_ref[...])
            pv = pv + _dot(jnp.where(own, spread, 0.0).astype(BF16), v)
        acc_ref[...] = alpha * acc_ref[...] + pv

    @pl.when(p < n_steps)
    def _():
        q = qbd_ref[...]
        s = jnp.concatenate([_dot(q, k[...].reshape(D_MODEL, PAGE).astype(BF16)) for k in k_refs], axis=1)
        accumulate(s, [v[...].reshape(PAGE * n_heads, LANES).astype(BF16) for v in v_refs])

    @pl.when(p == n_steps)
    def _():
        s = _dot(qbd_ref[...], knt_ref[...])
        s = jnp.where(_new_token_mask(s.shape, n_new), s, NEG)
        accumulate(s, [vn_ref[...]])
        accn = acc_ref[...] / l_ref[...]
        lam = _diff_lambda(lam_ref[...], lam_init)
        for h in range(n_heads):
            o = accn[h * hrows:h * hrows + ROWS] - lam * accn[h * hrows + ROWS:(h + 1) * hrows]
            o_ref[:, h * LANES:(h + 1) * LANES] = _rms(o, sg_ref[...]) * (1.0 - lam_init)


def _diff_sample(qbd, cache_kt, cache_v, knt, vn, pt, layer, lam_p, subln, lam_init, n_new):
    nb, n_rows, _ = qbd.shape
    n_pages = pt.shape[0] // nb
    assert n_pages % GP == 0
    n_steps = n_pages // GP
    per_b3 = lambda b, p, pt: (b, 0, 0)
    fixed = lambda b, p, pt: (0, 0)
    n_heads = D_MODEL // LANES
    expand = (lax.broadcasted_iota(I32, (PAGE, PAGE * n_heads), 0)
              == lax.broadcasted_iota(I32, (PAGE, PAGE * n_heads), 1) // n_heads).astype(BF16)
    grid_spec = pltpu.PrefetchScalarGridSpec(
        num_scalar_prefetch=1,
        grid=(nb, n_steps + 1),
        in_specs=[pl.BlockSpec((4, HEAD_DIM), fixed),
                  pl.BlockSpec((1, LANES), fixed),
                  pl.BlockSpec((None, n_rows, D_MODEL), per_b3),
                  pl.BlockSpec((PAGE, PAGE * n_heads), fixed)]
                 + _page_specs(cache_kt.shape[2:], GP, layer, n_pages)
                 + _page_specs(cache_v.shape[2:], GP, layer, n_pages)
                 + [pl.BlockSpec((None, D_MODEL, PAGE), per_b3),
                    pl.BlockSpec((None, PAGE * n_heads, LANES), per_b3)],
        out_specs=pl.BlockSpec((None, ROWS, D_MODEL), per_b3),
        scratch_shapes=[pltpu.VMEM((n_rows, 1), F32), pltpu.VMEM((n_rows, 1), F32),
                        pltpu.VMEM((n_rows, LANES), F32)])
    return pl.pallas_call(
        functools.partial(_diff_sample_kernel, lam_init=lam_init, n_steps=n_steps, n_new=n_new),
        grid_spec=grid_spec,
        out_shape=jax.ShapeDtypeStruct((nb, ROWS, D_MODEL), F32),
        compiler_params=_cparams(("arbitrary", "arbitrary")),
        name="diff_sample",
    )(pt, lam_p, subln, qbd, expand, *([cache_kt] * GP), *([cache_v] * GP), knt, vn)


QB = 128
CK = 512


def _dsa_prompt_kernel(qi_ref, wi_ref, ki_ref, q_ref, k_ref, vt_ref, o_ref, key_ref, bias_ref,
                       qs_ref, m_ref, l_ref, acc_ref, *, seq):
    i = pl.program_id(1)
    nch = ((i + 1) * QB + CK - 1) // CK
    kpos0 = lax.broadcasted_iota(I32, (CK, QB), 0)
    qpos = i * QB + lax.broadcasted_iota(I32, (CK, QB), 1)

    def index_chunk(c, carry):
        kic = ki_ref[pl.ds(pl.multiple_of(c * CK, CK), CK), :]
        s_all = _dot_nt(kic, qi_ref[...].reshape(N_HEADS_IDX * QB, HEAD_DIM))
        sc = jnp.zeros((CK, QB), F32)
        for h in range(N_HEADS_IDX):
            sc = sc + wi_ref[h:h + 1, :] * jnp.maximum(s_all[:, h * QB:(h + 1) * QB], 0.0)
        key_ref[c] = _sortable(jnp.where(c * CK + kpos0 <= qpos, sc, -jnp.inf))
        return carry

    lax.fori_loop(0, nch, index_chunk, 0)

    def count(pred):
        def body(c, acc):
            hit = pred(key_ref[c], c * CK + kpos0).astype(I32)
            return acc + hit.reshape(CK // 8, 8, QB).sum(axis=0)
        acc = lax.fori_loop(0, nch, body, jnp.zeros((8, QB), I32))
        return jnp.sum(acc, axis=0, keepdims=True)

    thr = _kth_largest(lambda t: count(lambda key, kpos: key >= t), (1, QB))
    need = TOPK - count(lambda key, kpos: key > thr)
    n_tie = count(lambda key, kpos: key == thr)
    all_ties = jnp.max(jnp.where(n_tie > need, 1, 0)) == 0
    nbits = (seq - 1).bit_length() + 1
    cut = lax.cond(
        all_ties,
        lambda: jnp.full((1, QB), 2 ** nbits - 1, I32),
        lambda: _tie_cutoff(lambda j: count(lambda key, kpos: (key == thr) & (kpos < j)), need, (1, QB), nbits))

    def bias_chunk(c, carry):
        key = key_ref[c]
        kpos = c * CK + kpos0
        sel = ((key > thr) | ((key == thr) & (kpos < cut))) & (kpos <= qpos)
        bias_ref[c] = jnp.where(sel, 0.0, NEG)
        return carry

    lax.fori_loop(0, nch, bias_chunk, 0)

    _init_softmax(m_ref, l_ref, acc_ref)
    lo = _lane_iota((1, LANES)) < HEAD_DIM
    n_pairs = D_MODEL // LANES
    for hp in range(n_pairs):
        q = q_ref[:, hp * LANES:(hp + 1) * LANES]
        qs_ref[hp] = jnp.concatenate([jnp.where(lo, q, jnp.zeros_like(q)), jnp.where(lo, jnp.zeros_like(q), q)],
                                     axis=0)

    def attend_chunk(c, carry):
        bias = bias_ref[c]
        off = pl.multiple_of(c * CK, CK)

        def pair_scores(hp):
            return _dot_nt(k_ref[pl.ds(off, CK), hp * LANES:(hp + 1) * LANES], qs_ref[hp])

        def accumulate(hp, alpha, pv):
            for j in range(2):
                rows = slice(hp * LANES + j * HEAD_DIM, hp * LANES + (j + 1) * HEAD_DIM)
                cols = slice(j * QB, (j + 1) * QB)
                acc_ref[rows, :] = alpha[:, cols] * acc_ref[rows, :] + pv[j * HEAD_DIM:(j + 1) * HEAD_DIM, cols]

        s_next = pair_scores(0)
        pending = None
        for hp in range(n_pairs):
            s = s_next
            if hp + 1 < n_pairs:
                s_next = pair_scores(hp + 1)
            s = jnp.concatenate([s[:, :QB] + bias, s[:, QB:] + bias], axis=1)
            m_old = m_ref[hp:hp + 1, :]
            m_new = jnp.maximum(m_old, jnp.max(s, axis=0, keepdims=True))
            alpha = jnp.exp2(m_old - m_new)
            pr = jnp.exp2(s - m_new)
            l_ref[hp:hp + 1, :] = alpha * l_ref[hp:hp + 1, :] + jnp.sum(pr, axis=0, keepdims=True)
            m_ref[hp:hp + 1, :] = m_new
            pv = _dot(vt_ref[c, hp * LANES:(hp + 1) * LANES, :], pr.astype(BF16))
            if pending is not None:
                accumulate(*pending)
            pending = (hp, alpha, pv)
        accumulate(*pending)
        return carry

    lax.fori_loop(0, nch, attend_chunk, 0)
    for hp in range(n_pairs):
        ot = jnp.concatenate(
            [acc_ref[hp * LANES + j * HEAD_DIM:hp * LANES + (j + 1) * HEAD_DIM, :]
             / l_ref[hp:hp + 1, j * QB:(j + 1) * QB] for j in range(2)], axis=0)
        o_ref[:, hp * LANES:(hp + 1) * LANES] = ot.T.astype(BF16)


def _dsa_prompt(qi, wi, ki, q, k, vt, batch):
    m = q.shape[0]
    t = m // batch
    nq = t // QB
    assert t % CK == 0 and CK % QB == 0
    blk = lambda b, i: (b * nq + i, 0)
    per_b = lambda b, i: (b, 0)
    return pl.pallas_call(
        functools.partial(_dsa_prompt_kernel, seq=t),
        grid=(batch, nq),
        in_specs=[pl.BlockSpec((N_HEADS_IDX, QB, HEAD_DIM), lambda b, i: (0, b * nq + i, 0)),
                  pl.BlockSpec((N_HEADS_IDX, QB), lambda b, i: (0, b * nq + i)),
                  pl.BlockSpec((t, HEAD_DIM), per_b),
                  pl.BlockSpec((QB, D_MODEL), blk),
                  pl.BlockSpec((t, D_MODEL), per_b),
                  pl.BlockSpec((None, t // CK, D_MODEL, CK), lambda b, i: (b, 0, 0, 0))],
        out_specs=pl.BlockSpec((QB, D_MODEL), blk),
        out_shape=jax.ShapeDtypeStruct((m, D_MODEL), BF16),
        scratch_shapes=[pltpu.VMEM((t // CK, CK, QB), I32), pltpu.VMEM((t // CK, CK, QB), F32),
                        pltpu.VMEM((D_MODEL // LANES, 2 * QB, LANES), BF16),
                        pltpu.VMEM((D_MODEL // LANES, 2 * QB), F32), pltpu.VMEM((D_MODEL // LANES, 2 * QB), F32),
                        pltpu.VMEM((D_MODEL, QB), F32)],
        compiler_params=_cparams(("arbitrary", "arbitrary")),
        name="dsa_prompt",
    )(qi, wi, ki, q, k, vt)


def _dsa_index_sample_kernel(pt_ref, qi_ref, wi_ref, *refs, n_pages, n_new):
    k_refs = refs[:GI]
    kn_ref, key_out, thr_out, cut_out, key_ref = refs[GI:]
    p = pl.program_id(1)
    n_steps = n_pages // GI

    def page_keys(kt, mask=None):
        s = jnp.maximum(_dot(qi_ref[...], kt), 0.0) * wi_ref[...]
        sc = s.reshape(N_HEADS_IDX, ROWS, PAGE).sum(axis=0)
        if mask is not None:
            sc = jnp.where(mask, sc, -jnp.inf)
        return _sortable(sc)

    @pl.when(p < n_steps)
    def _():
        for g, k in enumerate(k_refs):
            key = page_keys(k[...].astype(BF16))
            key_ref[p * GI + g] = key
            key_out[g] = key

    @pl.when(p == n_steps)
    def _():
        key = page_keys(kn_ref[...], _new_token_mask((ROWS, PAGE), n_new))
        key_ref[n_pages] = key
        key_out[0] = key
        key_out[1:] = jnp.full((GI - 1, ROWS, PAGE), INT_MIN, I32)
        keys = key_ref[...]
        idx = (lax.broadcasted_iota(I32, keys.shape, 0) * PAGE + lax.broadcasted_iota(I32, keys.shape, 2))

        def count(pred):
            return jnp.sum(jnp.sum(pred.astype(I32), axis=0), axis=-1, keepdims=True)

        thr = _kth_largest(lambda t: count(keys >= t[None]), (ROWS, 1))
        need = TOPK - count(keys > thr[None])
        nbits = ((n_pages + 1) * PAGE - 1).bit_length() + 1
        cut = _tie_cutoff(lambda j: count((keys == thr[None]) & (idx < j[None])), need, (ROWS, 1), nbits)
        thr_out[...] = jnp.broadcast_to(thr, (ROWS, LANES))
        cut_out[...] = jnp.broadcast_to(cut, (ROWS, LANES))


def _dsa_index_sample(qi, wi, cache_kit, kin_t, pt, layer, n_new):
    nb = qi.shape[0]
    n_pages = pt.shape[0] // nb
    assert n_pages % GI == 0
    n_steps = n_pages // GI
    per_b = lambda b, p, pt: (b, 0, 0)
    grid_spec = pltpu.PrefetchScalarGridSpec(
        num_scalar_prefetch=1,
        grid=(nb, n_steps + 1),
        in_specs=[pl.BlockSpec((None, N_HEADS_IDX * ROWS, HEAD_DIM), per_b),
                  pl.BlockSpec((None, N_HEADS_IDX * ROWS, PAGE), per_b)]
                 + _page_specs(cache_kit.shape[2:], GI, layer, n_pages)
                 + [pl.BlockSpec((None, HEAD_DIM, PAGE), per_b)],
        out_specs=[pl.BlockSpec((None, GI, ROWS, PAGE), lambda b, p, pt: (b, p, 0, 0)),
                   pl.BlockSpec((None, ROWS, LANES), per_b),
                   pl.BlockSpec((None, ROWS, LANES), per_b)],
        scratch_shapes=[pltpu.VMEM((n_pages + 1, ROWS, PAGE), I32)])
    return pl.pallas_call(
        functools.partial(_dsa_index_sample_kernel, n_pages=n_pages, n_new=n_new),
        grid_spec=grid_spec,
        out_shape=[jax.ShapeDtypeStruct((nb, n_pages + GI, ROWS, PAGE), I32),
                   jax.ShapeDtypeStruct((nb, ROWS, LANES), I32),
                   jax.ShapeDtypeStruct((nb, ROWS, LANES), I32)],
        compiler_params=_cparams(("arbitrary", "arbitrary")),
        name="dsa_index_sample",
    )(pt, qi, wi, *([cache_kit] * GI), kin_t)


def _dsa_sample_kernel(pt_ref, qbd_ref, key_in, thr_ref, cut_ref, *refs, n_steps, n_new):
    k_refs, v_refs = refs[:GP], refs[GP:2 * GP]
    knt_ref, vnt_ref, o_ref, m_ref, l_ref, acc_ref = refs[2 * GP:]
    p = pl.program_id(1)
    n_heads = D_MODEL // HEAD_DIM

    @pl.when(p == 0)
    def _():
        _init_softmax(m_ref, l_ref, acc_ref)

    def selected(g, extra=None):
        key = key_in[g]
        idx = (p * GP + g) * PAGE + lax.broadcasted_iota(I32, key.shape, 1)
        thr = thr_ref[...]
        sel = (key > thr) | ((key == thr) & (idx < cut_ref[...]))
        if extra is not None:
            sel = sel & extra
        return jnp.where(sel, 1, 0)

    def accumulate(s, sel, head_values):
        sel = jnp.concatenate([sel] * n_heads, axis=0) > 0
        alpha, pr = _softmax_update(jnp.where(sel, s, NEG), m_ref, l_ref)
        pv = []
        for h in range(n_heads):
            ph = pr[h * ROWS:(h + 1) * ROWS].astype(BF16)
            pv.append(sum(_dot_nt(ph[:, g * PAGE:(g + 1) * PAGE], vt) for g, vt in enumerate(head_values(h))))
        acc_ref[...] = alpha * acc_ref[...] + jnp.concatenate(pv, axis=0)

    @pl.when(p < n_steps)
    def _():
        q = qbd_ref[...]
        s = jnp.concatenate([_dot(q, k[...].reshape(D_MODEL, PAGE).astype(BF16)) for k in k_refs], axis=1)
        sel = jnp.concatenate([selected(g) for g in range(GP)], axis=1)
        accumulate(s, sel, lambda h: [v[h].astype(BF16) for v in v_refs])

    @pl.when(p == n_steps)
    def _():
        s = _dot(qbd_ref[...], knt_ref[...])
        accumulate(s, selected(0, _new_token_mask((ROWS, PAGE), n_new)), lambda h: [vnt_ref[h]])
        o_ref[...] = acc_ref[...] / l_ref[...]


def _dsa_sample(qbd, keys, thr, cut, cache_kt, cache_vt, knt, vnt, pt, layer, n_new):
    nb, n_rows, _ = qbd.shape
    n_pages = pt.shape[0] // nb
    assert n_pages % GP == 0
    n_steps = n_pages // GP
    per_b3 = lambda b, p, pt: (b, 0, 0)
    per_b4 = lambda b, p, pt: (b, 0, 0, 0)
    grid_spec = pltpu.PrefetchScalarGridSpec(
        num_scalar_prefetch=1,
        grid=(nb, n_steps + 1),
        in_specs=[pl.BlockSpec((None, n_rows, D_MODEL), per_b3),
                  pl.BlockSpec((None, GP, ROWS, PAGE), lambda b, p, pt: (b, p, 0, 0)),
                  pl.BlockSpec((None, ROWS, LANES), per_b3),
                  pl.BlockSpec((None, ROWS, LANES), per_b3)]
                 + _page_specs(cache_kt.shape[2:], GP, layer, n_pages)
                 + _page_specs(cache_vt.shape[2:], GP, layer, n_pages)
                 + [pl.BlockSpec((None, D_MODEL, PAGE), per_b3),
                    pl.BlockSpec((None,) + tuple(vnt.shape[1:]), per_b4)],
        out_specs=pl.BlockSpec((None, n_rows, HEAD_DIM), per_b3),
        scratch_shapes=[pltpu.VMEM((n_rows, 1), F32), pltpu.VMEM((n_rows, 1), F32),
                        pltpu.VMEM((n_rows, HEAD_DIM), F32)])
    return pl.pallas_call(
        functools.partial(_dsa_sample_kernel, n_steps=n_steps, n_new=n_new),
        grid_spec=grid_spec,
        out_shape=jax.ShapeDtypeStruct((nb, n_rows, HEAD_DIM), F32),
        compiler_params=_cparams(("arbitrary", "arbitrary")),
        name="dsa_sample",
    )(pt, qbd, keys, thr, cut, *([cache_kt] * GP), *([cache_vt] * GP), knt, vnt)


def _tile_gain(g):
    return jnp.tile(g.astype(F32), LANES // HEAD_DIM)[None, :]


def _pad_last(a, n):
    return jnp.pad(a, [(0, 0)] * (a.ndim - 1) + [(0, n - a.shape[-1])])


def kernel(x_prompt, x_sample, cache_a_k, cache_a_v, cache_b_k, cache_b_v, cache_b_kidx, page_table,
           p_prompt, p_sample, norm_mix, norm_ffn, norm_ple, a_w_qkv, a_w_o, a_q_norm, a_k_norm, a_lambda,
           a_subln, b_w_qkv, b_w_o, b_q_norm, b_k_norm, b_kidx_norm, w_ffn_in, w_ffn_out, w_ple_gate,
           w_ple_proj):
    batch, seq, _ = x_prompt.shape
    nb, n_new, _ = x_sample.shape
    depth = norm_mix.shape[0]
    page_size = cache_a_k.shape[2]
    n_pages = page_table.shape[1]
    past = n_pages * page_size
    mp, ms = batch * seq, nb * n_new
    tm_p = min(256, mp)
    n_sub = D_MODEL // HEAD_DIM

    xp = x_prompt.reshape(mp, D_MODEL)
    xs = x_sample.reshape(ms, D_MODEL)
    pt = page_table.reshape(-1).astype(I32)
    cos_p, sin_p = _rope_tables(jnp.arange(seq, dtype=I32))
    cos_s, sin_s = _rope_tables(jnp.tile(past + jnp.arange(n_new, dtype=I32), nb))
    ca_kt = cache_a_k.transpose(0, 1, 3, 4, 2)
    cb_kt = cache_b_k.transpose(0, 1, 3, 4, 2)
    cb_vt = cache_b_v.transpose(0, 1, 3, 4, 2)
    cb_kit = cache_b_kidx.transpose(0, 1, 3, 2)
    row_vec = lambda g: g.astype(F32)[None, :]

    def tokens8(a):
        a = a.reshape(nb, n_new, a.shape[-1])
        return jnp.pad(a, ((0, 0), (0, ROWS - n_new), (0, 0)))

    def keys_t(a):
        return _pad_last(a.reshape(nb, n_new, a.shape[-1]).transpose(0, 2, 1), PAGE)

    sub = jnp.arange(n_sub)
    own_a = (sub[None, None, :] == 2 * jnp.arange(n_sub // 2)[:, None, None] + jnp.arange(2)[None, :, None])
    own_b = sub[None, :] == sub[:, None]

    def qbd_a(q):
        q8 = tokens8(q).reshape(nb, 1, 1, ROWS, n_sub, HEAD_DIM)
        return jnp.where(own_a[None, :, :, None, :, None], q8, 0).reshape(nb, 2 * n_sub // 2 * ROWS, D_MODEL)

    def qbd_b(q):
        q8 = tokens8(q).reshape(nb, 1, ROWS, n_sub, HEAD_DIM)
        return jnp.where(own_b[None, :, None, :, None], q8, 0).reshape(nb, n_sub * ROWS, D_MODEL)

    outs = {name: [] for name in ("akp", "avp", "bkp", "bvp", "bkip", "aks", "avs", "bks", "bvs", "bkis")}
    for i in range(depth):
        j = i // 2
        gn = row_vec(norm_mix[i])
        if i % 2 == 0:
            lam_init = 0.8 - 0.6 * math.exp(-0.3 * i)
            w = a_w_qkv[j].astype(BF16)
            qg, kg = _tile_gain(a_q_norm[j]), _tile_gain(a_k_norm[j])
            lam_p, subln = a_lambda[j].astype(F32), row_vec(a_subln[j])
            wo = a_w_o[j].astype(BF16)
            q, k32, k16, v32, v16 = _proj_a(xp, gn, w, qg, kg, cos_p, sin_p, tm_p)
            vt = v16.reshape(batch, seq, D_MODEL).transpose(0, 2, 1).reshape(batch * D_MODEL, seq)
            op = _diff_prompt(q, k16, vt, lam_p, subln.reshape(LANES, 1), lam_init, batch,
                              min(1024, seq), min(512, seq))
            outs["akp"].append(k32)
            outs["avp"].append(v32)
            q, k32, k16, v32, v16 = _proj_a(xs, gn, w, qg, kg, cos_s, sin_s, ms)
            vn = jnp.pad(v16.reshape(nb, n_new, D_MODEL), ((0, 0), (0, PAGE - n_new), (0, 0)))
            vn = vn.reshape(nb, PAGE * n_sub // 2, LANES)
            os_ = _diff_sample(qbd_a(q), ca_kt, cache_a_v, keys_t(k16), vn, pt, j, lam_p, subln, lam_init, n_new)
            os_ = os_[:, :n_new].reshape(ms, D_MODEL).astype(BF16)
            outs["aks"].append(k32)
            outs["avs"].append(v32)
        else:
            w = b_w_qkv[j].astype(BF16)
            wt = _pad_last(w[:, B_TAIL:], LANES)
            w = w[:, :B_TAIL]
            qg, kg, kig = _tile_gain(b_q_norm[j]), _tile_gain(b_k_norm[j]), _tile_gain(b_kidx_norm[j])
            wo = b_w_o[j].astype(BF16)
            q, k32, k16, v32, v16, qi, tail, ki16 = _proj_b(xp, gn, w, wt, qg, kg, kig, cos_p, sin_p, tm_p)
            wi_t = tail[:, HEAD_DIM:HEAD_DIM + N_HEADS_IDX].T
            vt = v16.reshape(batch, seq // CK, CK, D_MODEL).transpose(0, 1, 3, 2)
            op = _dsa_prompt(qi, wi_t, ki16, q, k16, vt, batch)
            outs["bkp"].append(k32)
            outs["bvp"].append(v32)
            outs["bkip"].append(tail[:, :HEAD_DIM])
            q, k32, k16, v32, v16, qi, tail, ki16 = _proj_b(xs, gn, w, wt, qg, kg, kig, cos_s, sin_s, ms)
            qi8 = jnp.pad(qi.reshape(N_HEADS_IDX, nb, n_new, HEAD_DIM), ((0, 0), (0, 0), (0, ROWS - n_new), (0, 0)))
            qi8 = qi8.transpose(1, 0, 2, 3).reshape(nb, N_HEADS_IDX * ROWS, HEAD_DIM)
            wi8 = tokens8(tail[:, HEAD_DIM:HEAD_DIM + N_HEADS_IDX]).transpose(0, 2, 1)
            wi8 = jnp.broadcast_to(wi8.reshape(nb, N_HEADS_IDX * ROWS, 1), (nb, N_HEADS_IDX * ROWS, PAGE))
            keys, thr, cut = _dsa_index_sample(qi8, wi8, cb_kit, keys_t(ki16), pt, j, n_new)
            vnt = _pad_last(v16.reshape(nb, n_new, n_sub, HEAD_DIM).transpose(0, 2, 3, 1), PAGE)
            os_ = _dsa_sample(qbd_b(q), keys, thr, cut, cb_kt, cb_vt, keys_t(k16), vnt, pt, j, n_new)
            os_ = os_.reshape(nb, n_sub, ROWS, HEAD_DIM)[:, :, :n_new].transpose(0, 2, 1, 3)
            os_ = os_.reshape(ms, D_MODEL).astype(BF16)
            outs["bks"].append(k32)
            outs["bvs"].append(v32)
            outs["bkis"].append(tail[:, :HEAD_DIM])
        tail_w = (wo, row_vec(norm_ffn[i]), w_ffn_in[i].astype(BF16), w_ffn_out[i].astype(BF16),
                  row_vec(norm_ple[i]), w_ple_gate[i].astype(BF16), w_ple_proj[i].astype(BF16))
        xp = _tail(xp, op, p_prompt[i].reshape(mp, D_PLE), *tail_w, tm=tm_p)
        xs = _tail(xs, os_, p_sample[i].reshape(ms, D_PLE), *tail_w, tm=ms)

    n_a, n_b = D_MODEL // (2 * HEAD_DIM), D_MODEL // HEAD_DIM
    st = lambda name, shape: jnp.stack(outs[name]).reshape((len(outs[name]),) + shape)
    return (xp.reshape(batch, seq, D_MODEL), xs.reshape(nb, n_new, D_MODEL),
            st("akp", (batch, seq, 2 * n_a, HEAD_DIM)), st("avp", (batch, seq, n_a, 2 * HEAD_DIM)),
            st("bkp", (batch, seq, n_b, HEAD_DIM)), st("bvp", (batch, seq, n_b, HEAD_DIM)),
            st("bkip", (batch, seq, HEAD_DIM)),
            st("aks", (nb, n_new, 2 * n_a, HEAD_DIM)), st("avs", (nb, n_new, n_a, 2 * HEAD_DIM)),
            st("bks", (nb, n_new, n_b, HEAD_DIM)), st("bvs", (nb, n_new, n_b, HEAD_DIM)),
            st("bkis", (nb, n_new, HEAD_DIM)))
```

```python
import functools
import math

import jax
import jax.numpy as jnp
from jax import lax
from jax.experimental import pallas as pl
from jax.experimental.pallas import tpu as pltpu

F32 = jnp.float32
BF16 = jnp.bfloat16
I32 = jnp.int32

D_MODEL = 1024
HEAD_DIM = 64
LANES = 128
N_HEADS_IDX = 8
TOPK = 256
ROPE_THETA = 10000.0
RMS_EPS = 1e-6
D_FF = 2816
D_PLE = 256
PAGE = 128
NEG = -1e30
INT_MIN = -(2 ** 31)
VMEM_LIMIT = 56 * 1024 * 1024
Q_SCALE = HEAD_DIM ** -0.5 * math.log2(math.e)

_NT = (((1,), (1,)), ((), ()))


def _dot(a, b):
    return jnp.dot(a, b, preferred_element_type=F32)


def _dot_nt(a, b):
    return lax.dot_general(a, b, _NT, preferred_element_type=F32)


def _cparams(sem):
    return pltpu.CompilerParams(dimension_semantics=sem, vmem_limit_bytes=VMEM_LIMIT)


def _rms(x, g):
    return x * lax.rsqrt(jnp.mean(x * x, axis=-1, keepdims=True) + RMS_EPS) * g


def _lane_iota(shape):
    return lax.broadcasted_iota(I32, shape, len(shape) - 1)


def _head_norm(y, gmat, g):
    ss = _dot((y * y).astype(BF16), gmat)
    return y * lax.rsqrt(ss * (1.0 / HEAD_DIM) + RMS_EPS) * g


def _rope(y, cos, sin, first_half):
    rot = jnp.where(first_half, pltpu.roll(y, LANES - HEAD_DIM // 2, 1), pltpu.roll(y, HEAD_DIM // 2, 1))
    return y * cos + rot * sin


def _rope_tables(pos):
    half = HEAD_DIM // 2
    inv_freq = ROPE_THETA ** (-jnp.arange(half, dtype=F32) * 2.0 / HEAD_DIM)
    ang = pos.astype(F32)[:, None] * inv_freq[None, :]
    cos, sin = jnp.cos(ang), jnp.sin(ang)
    return (jnp.concatenate([cos, cos, cos, cos], axis=-1),
            jnp.concatenate([-sin, sin, -sin, sin], axis=-1))


def _group_matrix():
    r = lax.broadcasted_iota(I32, (LANES, LANES), 0) // HEAD_DIM
    c = lax.broadcasted_iota(I32, (LANES, LANES), 1) // HEAD_DIM
    return (r == c).astype(BF16)


def _sortable(score):
    score = jnp.where(score == 0.0, 0.0, score)
    bits = lax.bitcast_convert_type(score, I32)
    return bits ^ ((bits >> 31) & 0x7FFFFFFF)


def _kth_largest(count_ge, shape):
    t0 = jnp.where(count_ge(jnp.zeros(shape, I32)) >= TOPK, 0, INT_MIN).astype(I32)

    def body(b, t):
        cand = t | (1 << (30 - b))
        return jnp.where(count_ge(cand) >= TOPK, cand, t)

    return lax.fori_loop(0, 31, body, t0)


def _tie_cutoff(count_tie_lt, need, shape, nbits):
    def body(b, j):
        cand = j | (1 << (nbits - 1 - b))
        return jnp.where(count_tie_lt(cand) <= need, cand, j)

    return lax.fori_loop(0, nbits, body, jnp.zeros(shape, I32))


def _qkv_heads(h, w_ref, qg_ref, kg_ref, cos, sin, gmat, first, q_ref, k32_ref, k16_ref, v32_ref, v16_ref,
               key_minor=()):
    yq = _dot(h, w_ref[:, 0:D_MODEL])
    yk = _dot(h, w_ref[:, D_MODEL:2 * D_MODEL])
    for c in range(D_MODEL // LANES):
        sl = slice(c * LANES, (c + 1) * LANES)
        q = _rope(_head_norm(yq[:, sl], gmat, qg_ref[...]), cos, sin, first)
        q_ref[:, sl] = (q * Q_SCALE).astype(BF16)
        k = _rope(_head_norm(yk[:, sl], gmat, kg_ref[...]), cos, sin, first)
        k32_ref[:, sl] = k
        k16_ref[:, sl] = k.astype(BF16)
        if key_minor:
            key_minor[0][sl, :] = k.T
    v = _dot(h, w_ref[:, 2 * D_MODEL:3 * D_MODEL])
    v32_ref[...] = v
    v16_ref[...] = v.astype(BF16)
    if key_minor:
        for c in range(D_MODEL // LANES):
            sl = slice(c * LANES, (c + 1) * LANES)
            vt = v[:, sl].T
            key_minor[1][sl, :] = vt
            key_minor[2][sl, :] = vt.astype(BF16)


def _proj_a_kernel(x_ref, gn_ref, w_ref, qg_ref, kg_ref, cos_ref, sin_ref, gmat_ref,
                   q_ref, k32_ref, k16_ref, v32_ref, v16_ref, *key_minor):
    h = _rms(x_ref[...], gn_ref[...]).astype(BF16)
    first = (_lane_iota((1, LANES)) % HEAD_DIM) < HEAD_DIM // 2
    _qkv_heads(h, w_ref, qg_ref, kg_ref, cos_ref[...], sin_ref[...], gmat_ref[...], first,
               q_ref, k32_ref, k16_ref, v32_ref, v16_ref, key_minor)


def _key_minor_outputs(m, seq, tm, chunked):
    nt = seq // tm
    batch = m // seq
    flat = pl.BlockSpec((D_MODEL, tm), lambda i: (i // nt, i % nt))
    specs = [flat, flat, flat]
    shapes = [jax.ShapeDtypeStruct((batch * D_MODEL, seq), F32),
              jax.ShapeDtypeStruct((batch * D_MODEL, seq), F32),
              jax.ShapeDtypeStruct((batch * D_MODEL, seq), BF16)]
    if chunked:
        specs[2] = pl.BlockSpec((None, None, D_MODEL, tm), lambda i: (i // nt, i % nt, 0, 0))
        shapes[2] = jax.ShapeDtypeStruct((batch, nt, D_MODEL, tm), BF16)
    return specs, shapes


def _proj_a(x, gn, w, qg, kg, cos, sin, tm, key_minor=False):
    m = x.shape[0]
    nt = cos.shape[0] // tm
    row = lambda i: (i, 0)
    fixed = lambda i: (0, 0)
    wide = pl.BlockSpec((tm, D_MODEL), row)
    extra_specs, extra_shapes = _key_minor_outputs(m, cos.shape[0], tm, False) if key_minor else ([], [])
    return pl.pallas_call(
        _proj_a_kernel,
        grid=(m // tm,),
        in_specs=[wide,
                  pl.BlockSpec((1, D_MODEL), fixed),
                  pl.BlockSpec((D_MODEL, 3 * D_MODEL), fixed),
                  pl.BlockSpec((1, LANES), fixed),
                  pl.BlockSpec((1, LANES), fixed),
                  pl.BlockSpec((tm, LANES), lambda i: (i % nt, 0)),
                  pl.BlockSpec((tm, LANES), lambda i: (i % nt, 0)),
                  pl.BlockSpec((LANES, LANES), fixed)],
        out_specs=[wide, wide, wide, wide, wide] + extra_specs,
        out_shape=[jax.ShapeDtypeStruct((m, D_MODEL), BF16),
                   jax.ShapeDtypeStruct((m, D_MODEL), F32),
                   jax.ShapeDtypeStruct((m, D_MODEL), BF16),
                   jax.ShapeDtypeStruct((m, D_MODEL), F32),
                   jax.ShapeDtypeStruct((m, D_MODEL), BF16)] + extra_shapes,
        compiler_params=_cparams(("arbitrary",)),
        name="proj_a",
    )(x, gn, w, qg, kg, cos, sin, _group_matrix())


B_QI = 3 * D_MODEL
B_TAIL = B_QI + N_HEADS_IDX * HEAD_DIM


def _proj_b_kernel(x_ref, gn_ref, w_ref, wt_ref, qg_ref, kg_ref, kig_ref, cos_ref, sin_ref, gmat_ref,
                   q_ref, k32_ref, k16_ref, v32_ref, v16_ref, qi_ref, tail_ref, ki16_ref, *key_minor):
    h = _rms(x_ref[...], gn_ref[...]).astype(BF16)
    cos, sin = cos_ref[...], sin_ref[...]
    lane = _lane_iota((1, LANES))
    first = (lane % HEAD_DIM) < HEAD_DIM // 2
    _qkv_heads(h, w_ref, qg_ref, kg_ref, cos, sin, gmat_ref[...], first,
               q_ref, k32_ref, k16_ref, v32_ref, v16_ref, key_minor[:3])
    yi = _dot(h, w_ref[:, B_QI:B_TAIL])
    for c in range(N_HEADS_IDX * HEAD_DIM // LANES):
        qi = _rope(yi[:, c * LANES:(c + 1) * LANES], cos, sin, first) * HEAD_DIM ** -0.5
        qi_ref[2 * c] = qi[:, :HEAD_DIM].astype(BF16)
        qi_ref[2 * c + 1] = qi[:, HEAD_DIM:].astype(BF16)
    yt = _dot(h, wt_ref[...])
    is_key = lane < HEAD_DIM
    ms = jnp.sum(jnp.where(is_key, yt * yt, 0.0), axis=-1, keepdims=True) * (1.0 / HEAD_DIM)
    ki = _rope(yt * lax.rsqrt(ms + RMS_EPS) * kig_ref[...], cos, sin, first)
    tail = jnp.where(is_key, ki, yt * N_HEADS_IDX ** -0.5)
    tail_ref[...] = tail
    ki16_ref[...] = tail[:, :HEAD_DIM].astype(BF16)
    if key_minor:
        key_minor[3][...] = tail.T[:HEAD_DIM]


def _proj_b(x, gn, w, wt, qg, kg, kig, cos, sin, tm, key_minor=False):
    m = x.shape[0]
    seq = cos.shape[0]
    nt = seq // tm
    row = lambda i: (i, 0)
    fixed = lambda i: (0, 0)
    wide = pl.BlockSpec((tm, D_MODEL), row)
    extra_specs, extra_shapes = [], []
    if key_minor:
        extra_specs, extra_shapes = _key_minor_outputs(m, seq, tm, True)
        extra_specs.append(pl.BlockSpec((HEAD_DIM, tm), lambda i: (i // nt, i % nt)))
        extra_shapes.append(jax.ShapeDtypeStruct((m // seq * HEAD_DIM, seq), F32))
    return pl.pallas_call(
        _proj_b_kernel,
        grid=(m // tm,),
        in_specs=[wide,
                  pl.BlockSpec((1, D_MODEL), fixed),
                  pl.BlockSpec((D_MODEL, B_TAIL), fixed),
                  pl.BlockSpec((D_MODEL, LANES), fixed),
                  pl.BlockSpec((1, LANES), fixed),
                  pl.BlockSpec((1, LANES), fixed),
                  pl.BlockSpec((1, LANES), fixed),
                  pl.BlockSpec((tm, LANES), lambda i: (i % nt, 0)),
                  pl.BlockSpec((tm, LANES), lambda i: (i % nt, 0)),
                  pl.BlockSpec((LANES, LANES), fixed)],
        out_specs=[wide, wide, wide, wide, wide,
                   pl.BlockSpec((N_HEADS_IDX, tm, HEAD_DIM), lambda i: (0, i, 0)),
                   pl.BlockSpec((tm, LANES), row),
                   pl.BlockSpec((tm, HEAD_DIM), row)] + extra_specs,
        out_shape=[jax.ShapeDtypeStruct((m, D_MODEL), BF16),
                   jax.ShapeDtypeStruct((m, D_MODEL), F32),
                   jax.ShapeDtypeStruct((m, D_MODEL), BF16),
                   jax.ShapeDtypeStruct((m, D_MODEL), F32),
                   jax.ShapeDtypeStruct((m, D_MODEL), BF16),
                   jax.ShapeDtypeStruct((N_HEADS_IDX, m, HEAD_DIM), BF16),
                   jax.ShapeDtypeStruct((m, LANES), F32),
                   jax.ShapeDtypeStruct((m, HEAD_DIM), BF16)] + extra_shapes,
        compiler_params=_cparams(("arbitrary",)),
        name="proj_b",
    )(x, gn, w, wt, qg, kg, kig, cos, sin, _group_matrix())


FF_CHUNK = 256


def _tail_kernel(x_ref, o_ref, p_ref, wo_ref, gf_ref, win_ref, wout_ref, gp_ref, wg_ref, wp_ref,
                 y_ref, acc_ref):
    x1 = x_ref[...] + _dot(o_ref[...], wo_ref[...])
    h = _rms(x1, gf_ref[...]).astype(BF16)
    for c in range(D_FF // FF_CHUNK):
        g = _dot(h, win_ref[:, c * FF_CHUNK:(c + 1) * FF_CHUNK])
        u = _dot(h, win_ref[:, D_FF + c * FF_CHUNK:D_FF + (c + 1) * FF_CHUNK])
        a = (g * jax.nn.sigmoid(g) * u).astype(BF16)
        d = _dot(a, wout_ref[c * FF_CHUNK:(c + 1) * FF_CHUNK, :])
        if c == 0:
            acc_ref[...] = x1 + d
        else:
            acc_ref[...] += d
    x2 = acc_ref[...]
    gate = jax.nn.sigmoid(_dot(_rms(x2, gp_ref[...]).astype(BF16), wg_ref[...]))
    y_ref[...] = x2 + gate * _dot(p_ref[...].astype(BF16), wp_ref[...])


def _tail(x, o, p, wo, gf, win, wout, gp, wg, wp, tm):
    m = x.shape[0]
    row = lambda i: (i, 0)
    fixed = lambda i: (0, 0)
    once = dict(pipeline_mode=pl.Buffered(1))
    return pl.pallas_call(
        _tail_kernel,
        grid=(m // tm,),
        in_specs=[pl.BlockSpec((tm, D_MODEL), row),
                  pl.BlockSpec((tm, D_MODEL), row),
                  pl.BlockSpec((tm, D_PLE), row),
                  pl.BlockSpec((D_MODEL, D_MODEL), fixed, **once),
                  pl.BlockSpec((1, D_MODEL), fixed),
                  pl.BlockSpec((D_MODEL, 2 * D_FF), fixed, **once),
                  pl.BlockSpec((D_FF, D_MODEL), fixed, **once),
                  pl.BlockSpec((1, D_MODEL), fixed),
                  pl.BlockSpec((D_MODEL, D_MODEL), fixed, **once),
                  pl.BlockSpec((D_PLE, D_MODEL), fixed, **once)],
        out_specs=pl.BlockSpec((tm, D_MODEL), row),
        out_shape=jax.ShapeDtypeStruct((m, D_MODEL), F32),
        scratch_shapes=[pltpu.VMEM((tm, D_MODEL), F32)],
        compiler_params=_cparams(("arbitrary",)),
        name="tail",
    )(x, o, p, wo, gf, win, wout, gp, wg, wp)


def _diff_lambda(lp, lam_init):
    a = jnp.sum(lp[0:1] * lp[1:2], axis=-1, keepdims=True)
    b = jnp.sum(lp[2:3] * lp[3:4], axis=-1, keepdims=True)
    return jnp.exp(a) - jnp.exp(b) + lam_init


def _softmax_update(s, m_ref, l_ref):
    m_old = m_ref[...]
    m_new = jnp.maximum(m_old, jnp.max(s, axis=-1, keepdims=True))
    alpha = jnp.exp2(m_old - m_new)
    pr = jnp.exp2(s - m_new)
    l_ref[...] = alpha * l_ref[...] + jnp.sum(pr, axis=-1, keepdims=True)
    m_ref[...] = m_new
    return alpha, pr


def _init_softmax(m_ref, l_ref, acc_ref):
    m_ref[...] = jnp.full(m_ref.shape, NEG, F32)
    l_ref[...] = jnp.zeros(l_ref.shape, F32)
    acc_ref[...] = jnp.zeros(acc_ref.shape, F32)


def _diff_prompt_kernel(qt_ref, kt_ref, lam_ref, sg_ref, q_ref, k_ref, vt_ref, o_ref,
                        qs_ref, m_ref, l_ref, acc_ref, *, lam_init, tq, tk):
    step = pl.program_id(2)
    qi, ki = qt_ref[step], kt_ref[step]

    @pl.when(ki == 0)
    def _():
        _init_softmax(m_ref, l_ref, acc_ref)
        q = q_ref[...]
        lo = _lane_iota((1, LANES)) < HEAD_DIM
        qs_ref[...] = jnp.concatenate([jnp.where(lo, q, jnp.zeros_like(q)), jnp.where(lo, jnp.zeros_like(q), q)],
                                      axis=0)

    def scores(masked):
        k, vt = k_ref[...], vt_ref[...]
        sub = min(tq, 512)
        n_sub = 2 * tq // sub

        def sub_scores(g):
            return _dot_nt(k, qs_ref[g * sub:(g + 1) * sub, :])

        def accumulate(g, alpha, pv):
            cols = slice(g * sub, (g + 1) * sub)
            acc_ref[:, cols] = alpha * acc_ref[:, cols] + pv

        s_next = sub_scores(0)
        pending = None
        for g in range(n_sub):
            s = s_next
            if g + 1 < n_sub:
                s_next = sub_scores(g + 1)
            if masked:
                kpos = ki * tk + lax.broadcasted_iota(I32, (tk, sub), 0)
                qpos = qi * tq + (g * sub) % tq + lax.broadcasted_iota(I32, (tk, sub), 1)
                s = jnp.where(kpos <= qpos, s, NEG)
            cols = slice(g * sub, (g + 1) * sub)
            m_old = m_ref[:, cols]
            m_new = jnp.maximum(m_old, jnp.max(s, axis=0, keepdims=True))
            alpha = jnp.exp2(m_old - m_new)
            pr = jnp.exp2(s - m_new)
            l_ref[:, cols] = alpha * l_ref[:, cols] + jnp.sum(pr, axis=0, keepdims=True)
            m_ref[:, cols] = m_new
            pv = _dot(vt, pr.astype(BF16))
            if pending is not None:
                accumulate(*pending)
            pending = (g, alpha, pv)
        accumulate(*pending)

    straddles = ki * tk + tk - 1 > qi * tq
    pl.when(jnp.logical_not(straddles))(lambda: scores(False))
    pl.when(straddles)(lambda: scores(True))

    @pl.when((ki + 1) * tk >= (qi + 1) * tq)
    def _():
        lam = _diff_lambda(lam_ref[...], lam_init)
        a = acc_ref[...] / l_ref[...]
        o = a[:, :tq] - lam * a[:, tq:]
        o = o * lax.rsqrt(jnp.mean(o * o, axis=0, keepdims=True) + RMS_EPS) * sg_ref[...]
        o_ref[...] = (o * (1.0 - lam_init)).T.astype(BF16)


def _diff_prompt(q, k, vt, lam_p, subln_col, lam_init, batch, tq, tk):
    m = q.shape[0]
    t = m // batch
    nq = t // tq
    pairs = [(a, b) for a in range(nq) for b in range(((a + 1) * tq + tk - 1) // tk)]
    qt = jnp.array([a for a, _ in pairs], I32)
    kt = jnp.array([b for _, b in pairs], I32)
    n_heads = D_MODEL // LANES
    qmap = lambda b, h, s, qt, kt: (b * nq + qt[s], h)
    kmap = lambda b, h, s, qt, kt: (b * (t // tk) + kt[s], h)
    grid_spec = pltpu.PrefetchScalarGridSpec(
        num_scalar_prefetch=2,
        grid=(batch, n_heads, len(pairs)),
        in_specs=[pl.BlockSpec((4, HEAD_DIM), lambda b, h, s, qt, kt: (0, 0)),
                  pl.BlockSpec((LANES, 1), lambda b, h, s, qt, kt: (0, 0)),
                  pl.BlockSpec((tq, LANES), qmap),
                  pl.BlockSpec((tk, LANES), kmap),
                  pl.BlockSpec((LANES, tk), lambda b, h, s, qt, kt: (b * n_heads + h, kt[s]))],
        out_specs=pl.BlockSpec((tq, LANES), qmap),
        scratch_shapes=[pltpu.VMEM((2 * tq, LANES), BF16),
                        pltpu.VMEM((1, 2 * tq), F32), pltpu.VMEM((1, 2 * tq), F32),
                        pltpu.VMEM((LANES, 2 * tq), F32)])
    return pl.pallas_call(
        functools.partial(_diff_prompt_kernel, lam_init=lam_init, tq=tq, tk=tk),
        grid_spec=grid_spec,
        out_shape=jax.ShapeDtypeStruct((m, D_MODEL), BF16),
        compiler_params=_cparams(("arbitrary", "arbitrary", "arbitrary")),
        name="diff_prompt",
    )(qt, kt, lam_p, subln_col, q, k, vt)


ROWS = 8
GP = 4
GI = 16


def _page_map(g, group, layer, n_pages, ndim):
    def index_map(b, p, pt):
        page = pt[b * n_pages + jnp.minimum(p * group + g, n_pages - 1)]
        return (layer, page) + (0,) * ndim
    return index_map


def _page_specs(block, group, layer, n_pages):
    return [pl.BlockSpec((None, None) + tuple(block), _page_map(g, group, layer, n_pages, len(block)))
            for g in range(group)]


def _new_token_mask(shape, n_new):
    tok = lax.broadcasted_iota(I32, shape, 0) % ROWS
    col = lax.broadcasted_iota(I32, shape, 1)
    return (col <= tok) & (col < n_new)


def _diff_sample_kernel(pt_ref, lam_ref, sg_ref, qbd_ref, expand_ref, *refs, lam_init, n_steps, n_new):
    k_refs, v_refs = refs[:GP], refs[GP:2 * GP]
    knt_ref, vn_ref, o_ref, m_ref, l_ref, acc_ref = refs[2 * GP:]
    p = pl.program_id(1)
    n_heads = D_MODEL // LANES
    hrows = 2 * ROWS

    @pl.when(p == 0)
    def _():
        _init_softmax(m_ref, l_ref, acc_ref)

    def accumulate(s, values):
        alpha, pr = _softmax_update(s, m_ref, l_ref)
        shape = (n_heads * hrows, PAGE * n_heads)
        own = lax.broadcasted_iota(I32, shape, 0) // hrows == lax.broadcasted_iota(I32, shape, 1) % n_heads
        pv = 0.0
        for g, v in enumerate(values):
            spread = _dot(pr[:, g * PAGE:(g + 1) * PAGE].astype(BF16), expand_ref[...])
            pv = pv + _dot(jnp.where(own, spread, 0.0).astype(BF16), v)
        acc_ref[...] = alpha * acc_ref[...] + pv

    @pl.when(p < n_steps)
    def _():
        q = qbd_ref[...]
        s = jnp.concatenate([_dot(q, k[...].reshape(D_MODEL, PAGE).astype(BF16)) for k in k_refs], axis=1)
        accumulate(s, [v[...].reshape(PAGE * n_heads, LANES).astype(BF16) for v in v_refs])

    @pl.when(p == n_steps)
    def _():
        s = _dot(qbd_ref[...], knt_ref[...])
        s = jnp.where(_new_token_mask(s.shape, n_new), s, NEG)
        accumulate(s, [vn_ref[...]])
        accn = acc_ref[...] / l_ref[...]
        lam = _diff_lambda(lam_ref[...], lam_init)
        for h in range(n_heads):
            o = accn[h * hrows:h * hrows + ROWS] - lam * accn[h * hrows + ROWS:(h + 1) * hrows]
            o_ref[:, h * LANES:(h + 1) * LANES] = _rms(o, sg_ref[...]) * (1.0 - lam_init)


def _diff_sample(qbd, cache_kt, cache_v, knt, vn, pt, layer, lam_p, subln, lam_init, n_new):
    nb, n_rows, _ = qbd.shape
    n_pages = pt.shape[0] // nb
    assert n_pages % GP == 0
    n_steps = n_pages // GP
    per_b3 = lambda b, p, pt: (b, 0, 0)
    fixed = lambda b, p, pt: (0, 0)
    n_heads = D_MODEL // LANES
    expand = (lax.broadcasted_iota(I32, (PAGE, PAGE * n_heads), 0)
              == lax.broadcasted_iota(I32, (PAGE, PAGE * n_heads), 1) // n_heads).astype(BF16)
    grid_spec = pltpu.PrefetchScalarGridSpec(
        num_scalar_prefetch=1,
        grid=(nb, n_steps + 1),
        in_specs=[pl.BlockSpec((4, HEAD_DIM), fixed),
                  pl.BlockSpec((1, LANES), fixed),
                  pl.BlockSpec((None, n_rows, D_MODEL), per_b3),
                  pl.BlockSpec((PAGE, PAGE * n_heads), fixed)]
                 + _page_specs(cache_kt.shape[2:], GP, layer, n_pages)
                 + _page_specs(cache_v.shape[2:], GP, layer, n_pages)
                 + [pl.BlockSpec((None, D_MODEL, PAGE), per_b3),
                    pl.BlockSpec((None, PAGE * n_heads, LANES), per_b3)],
        out_specs=pl.BlockSpec((None, ROWS, D_MODEL), per_b3),
        scratch_shapes=[pltpu.VMEM((n_rows, 1), F32), pltpu.VMEM((n_rows, 1), F32),
                        pltpu.VMEM((n_rows, LANES), F32)])
    return pl.pallas_call(
        functools.partial(_diff_sample_kernel, lam_init=lam_init, n_steps=n_steps, n_new=n_new),
        grid_spec=grid_spec,
        out_shape=jax.ShapeDtypeStruct((nb, ROWS, D_MODEL), F32),
        compiler_params=_cparams(("arbitrary", "arbitrary")),
        name="diff_sample",
    )(pt, lam_p, subln, qbd, expand, *([cache_kt] * GP), *([cache_v] * GP), knt, vn)


QB = 128
CK = 512


def _dsa_prompt_kernel(qi_ref, wi_ref, ki_ref, q_ref, k_ref, vt_ref, o_ref, key_ref, bias_ref,
                       qs_ref, m_ref, l_ref, acc_ref, *, seq):
    i = pl.program_id(1)
    nch = ((i + 1) * QB + CK - 1) // CK
    kpos0 = lax.broadcasted_iota(I32, (CK, QB), 0)
    qpos = i * QB + lax.broadcasted_iota(I32, (CK, QB), 1)

    def index_chunk(c, carry):
        kic = ki_ref[pl.ds(pl.multiple_of(c * CK, CK), CK), :]
        s_all = _dot_nt(kic, qi_ref[...].reshape(N_HEADS_IDX * QB, HEAD_DIM))
        sc = jnp.zeros((CK, QB), F32)
        for h in range(N_HEADS_IDX):
            sc = sc + wi_ref[h:h + 1, :] * jnp.maximum(s_all[:, h * QB:(h + 1) * QB], 0.0)
        key_ref[c] = _sortable(jnp.where(c * CK + kpos0 <= qpos, sc, -jnp.inf))
        return carry

    lax.fori_loop(0, nch, index_chunk, 0)

    def count(pred):
        def body(c, acc):
            hit = pred(key_ref[c], c * CK + kpos0).astype(I32)
            return acc + hit.reshape(CK // 8, 8, QB).sum(axis=0)
        acc = lax.fori_loop(0, nch, body, jnp.zeros((8, QB), I32))
        return jnp.sum(acc, axis=0, keepdims=True)

    thr = _kth_largest(lambda t: count(lambda key, kpos: key >= t), (1, QB))
    need = TOPK - count(lambda key, kpos: key > thr)
    n_tie = count(lambda key, kpos: key == thr)
    all_ties = jnp.max(jnp.where(n_tie > need, 1, 0)) == 0
    nbits = (seq - 1).bit_length() + 1
    cut = lax.cond(
        all_ties,
        lambda: jnp.full((1, QB), 2 ** nbits - 1, I32),
        lambda: _tie_cutoff(lambda j: count(lambda key, kpos: (key == thr) & (kpos < j)), need, (1, QB), nbits))

    def bias_chunk(c, carry):
        key = key_ref[c]
        kpos = c * CK + kpos0
        sel = ((key > thr) | ((key == thr) & (kpos < cut))) & (kpos <= qpos)
        bias_ref[c] = jnp.where(sel, 0.0, NEG)
        return carry

    lax.fori_loop(0, nch, bias_chunk, 0)

    _init_softmax(m_ref, l_ref, acc_ref)
    lo = _lane_iota((1, LANES)) < HEAD_DIM
    n_pairs = D_MODEL // LANES
    for hp in range(n_pairs):
        q = q_ref[:, hp * LANES:(hp + 1) * LANES]
        qs_ref[hp] = jnp.concatenate([jnp.where(lo, q, jnp.zeros_like(q)), jnp.where(lo, jnp.zeros_like(q), q)],
                                     axis=0)

    def attend_chunk(c, carry):
        bias = bias_ref[c]
        off = pl.multiple_of(c * CK, CK)

        def pair_scores(hp):
            return _dot_nt(k_ref[pl.ds(off, CK), hp * LANES:(hp + 1) * LANES], qs_ref[hp])

        def accumulate(hp, alpha, pv):
            for j in range(2):
                rows = slice(hp * LANES + j * HEAD_DIM, hp * LANES + (j + 1) * HEAD_DIM)
                cols = slice(j * QB, (j + 1) * QB)
                acc_ref[rows, :] = alpha[:, cols] * acc_ref[rows, :] + pv[j * HEAD_DIM:(j + 1) * HEAD_DIM, cols]

        s_next = pair_scores(0)
        pending = None
        for hp in range(n_pairs):
            s = s_next
            if hp + 1 < n_pairs:
                s_next = pair_scores(hp + 1)
            s = jnp.concatenate([s[:, :QB] + bias, s[:, QB:] + bias], axis=1)
            m_old = m_ref[hp:hp + 1, :]
            m_new = jnp.maximum(m_old, jnp.max(s, axis=0, keepdims=True))
            alpha = jnp.exp2(m_old - m_new)
            pr = jnp.exp2(s - m_new)
            l_ref[hp:hp + 1, :] = alpha * l_ref[hp:hp + 1, :] + jnp.sum(pr, axis=0, keepdims=True)
            m_ref[hp:hp + 1, :] = m_new
            pv = _dot(vt_ref[c, hp * LANES:(hp + 1) * LANES, :], pr.astype(BF16))
            if pending is not None:
                accumulate(*pending)
            pending = (hp, alpha, pv)
        accumulate(*pending)
        return carry

    lax.fori_loop(0, nch, attend_chunk, 0)
    for hp in range(n_pairs):
        ot = jnp.concatenate(
            [acc_ref[hp * LANES + j * HEAD_DIM:hp * LANES + (j + 1) * HEAD_DIM, :]
             / l_ref[hp:hp + 1, j * QB:(j + 1) * QB] for j in range(2)], axis=0)
        o_ref[:, hp * LANES:(hp + 1) * LANES] = ot.T.astype(BF16)


def _dsa_prompt(qi, wi, ki, q, k, vt, batch):
    m = q.shape[0]
    t = m // batch
    nq = t // QB
    assert t % CK == 0 and CK % QB == 0
    blk = lambda b, i: (b * nq + i, 0)
    per_b = lambda b, i: (b, 0)
    return pl.pallas_call(
        functools.partial(_dsa_prompt_kernel, seq=t),
        grid=(batch, nq),
        in_specs=[pl.BlockSpec((N_HEADS_IDX, QB, HEAD_DIM), lambda b, i: (0, b * nq + i, 0)),
                  pl.BlockSpec((N_HEADS_IDX, QB), lambda b, i: (0, b * nq + i)),
                  pl.BlockSpec((t, HEAD_DIM), per_b),
                  pl.BlockSpec((QB, D_MODEL), blk),
                  pl.BlockSpec((t, D_MODEL), per_b),
                  pl.BlockSpec((None, t // CK, D_MODEL, CK), lambda b, i: (b, 0, 0, 0))],
        out_specs=pl.BlockSpec((QB, D_MODEL), blk),
        out_shape=jax.ShapeDtypeStruct((m, D_MODEL), BF16),
        scratch_shapes=[pltpu.VMEM((t // CK, CK, QB), I32), pltpu.VMEM((t // CK, CK, QB), F32),
                        pltpu.VMEM((D_MODEL // LANES, 2 * QB, LANES), BF16),
                        pltpu.VMEM((D_MODEL // LANES, 2 * QB), F32), pltpu.VMEM((D_MODEL // LANES, 2 * QB), F32),
                        pltpu.VMEM((D_MODEL, QB), F32)],
        compiler_params=_cparams(("arbitrary", "arbitrary")),
        name="dsa_prompt",
    )(qi, wi, ki, q, k, vt)


def _dsa_index_sample_kernel(pt_ref, qi_ref, wi_ref, *refs, n_pages, n_new):
    k_refs = refs[:GI]
    kn_ref, key_out, thr_out, cut_out, key_ref = refs[GI:]
    p = pl.program_id(1)
    n_steps = n_pages // GI

    def page_keys(kt, mask=None):
        s = jnp.maximum(_dot(qi_ref[...], kt), 0.0) * wi_ref[...]
        sc = s.reshape(N_HEADS_IDX, ROWS, PAGE).sum(axis=0)
        if mask is not None:
            sc = jnp.where(mask, sc, -jnp.inf)
        return _sortable(sc)

    @pl.when(p < n_steps)
    def _():
        for g, k in enumerate(k_refs):
            key = page_keys(k[...].astype(BF16))
            key_ref[p * GI + g] = key
            key_out[g] = key

    @pl.when(p == n_steps)
    def _():
        key = page_keys(kn_ref[...], _new_token_mask((ROWS, PAGE), n_new))
        key_ref[n_pages] = key
        key_out[0] = key
        key_out[1:] = jnp.full((GI - 1, ROWS, PAGE), INT_MIN, I32)
        keys = key_ref[...]
        idx = (lax.broadcasted_iota(I32, keys.shape, 0) * PAGE + lax.broadcasted_iota(I32, keys.shape, 2))

        def count(pred):
            return jnp.sum(jnp.sum(pred.astype(I32), axis=0), axis=-1, keepdims=True)

        thr = _kth_largest(lambda t: count(keys >= t[None]), (ROWS, 1))
        need = TOPK - count(keys > thr[None])
        nbits = ((n_pages + 1) * PAGE - 1).bit_length() + 1
        cut = _tie_cutoff(lambda j: count((keys == thr[None]) & (idx < j[None])), need, (ROWS, 1), nbits)
        thr_out[...] = jnp.broadcast_to(thr, (ROWS, LANES))
        cut_out[...] = jnp.broadcast_to(cut, (ROWS, LANES))


def _dsa_index_sample(qi, wi, cache_kit, kin_t, pt, layer, n_new):
    nb = qi.shape[0]
    n_pages = pt.shape[0] // nb
    assert n_pages % GI == 0
    n_steps = n_pages // GI
    per_b = lambda b, p, pt: (b, 0, 0)
    grid_spec = pltpu.PrefetchScalarGridSpec(
        num_scalar_prefetch=1,
        grid=(nb, n_steps + 1),
        in_specs=[pl.BlockSpec((None, N_HEADS_IDX * ROWS, HEAD_DIM), per_b),
                  pl.BlockSpec((None, N_HEADS_IDX * ROWS, PAGE), per_b)]
                 + _page_specs(cache_kit.shape[2:], GI, layer, n_pages)
                 + [pl.BlockSpec((None, HEAD_DIM, PAGE), per_b)],
        out_specs=[pl.BlockSpec((None, GI, ROWS, PAGE), lambda b, p, pt: (b, p, 0, 0)),
                   pl.BlockSpec((None, ROWS, LANES), per_b),
                   pl.BlockSpec((None, ROWS, LANES), per_b)],
        scratch_shapes=[pltpu.VMEM((n_pages + 1, ROWS, PAGE), I32)])
    return pl.pallas_call(
        functools.partial(_dsa_index_sample_kernel, n_pages=n_pages, n_new=n_new),
        grid_spec=grid_spec,
        out_shape=[jax.ShapeDtypeStruct((nb, n_pages + GI, ROWS, PAGE), I32),
                   jax.ShapeDtypeStruct((nb, ROWS, LANES), I32),
                   jax.ShapeDtypeStruct((nb, ROWS, LANES), I32)],
        compiler_params=_cparams(("arbitrary", "arbitrary")),
        name="dsa_index_sample",
    )(pt, qi, wi, *([cache_kit] * GI), kin_t)


def _dsa_sample_kernel(pt_ref, qbd_ref, key_in, thr_ref, cut_ref, *refs, n_steps, n_new):
    k_refs, v_refs = refs[:GP], refs[GP:2 * GP]
    knt_ref, vnt_ref, o_ref, m_ref, l_ref, acc_ref = refs[2 * GP:]
    p = pl.program_id(1)
    n_heads = D_MODEL // HEAD_DIM

    @pl.when(p == 0)
    def _():
        _init_softmax(m_ref, l_ref, acc_ref)

    def selected(g, extra=None):
        key = key_in[g]
        idx = (p * GP + g) * PAGE + lax.broadcasted_iota(I32, key.shape, 1)
        thr = thr_ref[...]
        sel = (key > thr) | ((key == thr) & (idx < cut_ref[...]))
        if extra is not None:
            sel = sel & extra
        return jnp.where(sel, 1, 0)

    def accumulate(s, sel, head_values):
        sel = jnp.concatenate([sel] * n_heads, axis=0) > 0
        alpha, pr = _softmax_update(jnp.where(sel, s, NEG), m_ref, l_ref)
        pv = []
        for h in range(n_heads):
            ph = pr[h * ROWS:(h + 1) * ROWS].astype(BF16)
            pv.append(sum(_dot_nt(ph[:, g * PAGE:(g + 1) * PAGE], vt) for g, vt in enumerate(head_values(h))))
        acc_ref[...] = alpha * acc_ref[...] + jnp.concatenate(pv, axis=0)

    @pl.when(p < n_steps)
    def _():
        q = qbd_ref[...]
        s = jnp.concatenate([_dot(q, k[...].reshape(D_MODEL, PAGE).astype(BF16)) for k in k_refs], axis=1)
        sel = jnp.concatenate([selected(g) for g in range(GP)], axis=1)
        accumulate(s, sel, lambda h: [v[h].astype(BF16) for v in v_refs])

    @pl.when(p == n_steps)
    def _():
        s = _dot(qbd_ref[...], knt_ref[...])
        accumulate(s, selected(0, _new_token_mask((ROWS, PAGE), n_new)), lambda h: [vnt_ref[h]])
        o_ref[...] = acc_ref[...] / l_ref[...]


def _dsa_sample(qbd, keys, thr, cut, cache_kt, cache_vt, knt, vnt, pt, layer, n_new):
    nb, n_rows, _ = qbd.shape
    n_pages = pt.shape[0] // nb
    assert n_pages % GP == 0
    n_steps = n_pages // GP
    per_b3 = lambda b, p, pt: (b, 0, 0)
    per_b4 = lambda b, p, pt: (b, 0, 0, 0)
    grid_spec = pltpu.PrefetchScalarGridSpec(
        num_scalar_prefetch=1,
        grid=(nb, n_steps + 1),
        in_specs=[pl.BlockSpec((None, n_rows, D_MODEL), per_b3),
                  pl.BlockSpec((None, GP, ROWS, PAGE), lambda b, p, pt: (b, p, 0, 0)),
                  pl.BlockSpec((None, ROWS, LANES), per_b3),
                  pl.BlockSpec((None, ROWS, LANES), per_b3)]
                 + _page_specs(cache_kt.shape[2:], GP, layer, n_pages)
                 + _page_specs(cache_vt.shape[2:], GP, layer, n_pages)
                 + [pl.BlockSpec((None, D_MODEL, PAGE), per_b3),
                    pl.BlockSpec((None,) + tuple(vnt.shape[1:]), per_b4)],
        out_specs=pl.BlockSpec((None, n_rows, HEAD_DIM), per_b3),
        scratch_shapes=[pltpu.VMEM((n_rows, 1), F32), pltpu.VMEM((n_rows, 1), F32),
                        pltpu.VMEM((n_rows, HEAD_DIM), F32)])
    return pl.pallas_call(
        functools.partial(_dsa_sample_kernel, n_steps=n_steps, n_new=n_new),
        grid_spec=grid_spec,
        out_shape=jax.ShapeDtypeStruct((nb, n_rows, HEAD_DIM), F32),
        compiler_params=_cparams(("arbitrary", "arbitrary")),
        name="dsa_sample",
    )(pt, qbd, keys, thr, cut, *([cache_kt] * GP), *([cache_vt] * GP), knt, vnt)


def _tile_gain(g):
    return jnp.tile(g.astype(F32), LANES // HEAD_DIM)[None, :]


def _pad_last(a, n):
    return jnp.pad(a, [(0, 0)] * (a.ndim - 1) + [(0, n - a.shape[-1])])


def kernel(x_prompt, x_sample, cache_a_k, cache_a_v, cache_b_k, cache_b_v, cache_b_kidx, page_table,
           p_prompt, p_sample, norm_mix, norm_ffn, norm_ple, a_w_qkv, a_w_o, a_q_norm, a_k_norm, a_lambda,
           a_subln, b_w_qkv, b_w_o, b_q_norm, b_k_norm, b_kidx_norm, w_ffn_in, w_ffn_out, w_ple_gate,
           w_ple_proj):
    batch, seq, _ = x_prompt.shape
    nb, n_new, _ = x_sample.shape
    depth = norm_mix.shape[0]
    page_size = cache_a_k.shape[2]
    n_pages = page_table.shape[1]
    past = n_pages * page_size
    mp, ms = batch * seq, nb * n_new
    tm_p = min(512, mp)
    n_sub = D_MODEL // HEAD_DIM

    xp = x_prompt.reshape(mp, D_MODEL)
    xs = x_sample.reshape(ms, D_MODEL)
    pt = page_table.reshape(-1).astype(I32)
    cos_p, sin_p = _rope_tables(jnp.arange(seq, dtype=I32))
    cos_s, sin_s = _rope_tables(jnp.tile(past + jnp.arange(n_new, dtype=I32), nb))
    ca_kt = cache_a_k.transpose(0, 1, 3, 4, 2)
    cb_kt = cache_b_k.transpose(0, 1, 3, 4, 2)
    cb_vt = cache_b_v.transpose(0, 1, 3, 4, 2)
    cb_kit = cache_b_kidx.transpose(0, 1, 3, 2)
    row_vec = lambda g: g.astype(F32)[None, :]

    def tokens8(a):
        a = a.reshape(nb, n_new, a.shape[-1])
        return jnp.pad(a, ((0, 0), (0, ROWS - n_new), (0, 0)))

    def keys_t(a):
        return _pad_last(a.reshape(nb, n_new, a.shape[-1]).transpose(0, 2, 1), PAGE)

    sub = jnp.arange(n_sub)
    own_a = (sub[None, None, :] == 2 * jnp.arange(n_sub // 2)[:, None, None] + jnp.arange(2)[None, :, None])
    own_b = sub[None, :] == sub[:, None]

    def qbd_a(q):
        q8 = tokens8(q).reshape(nb, 1, 1, ROWS, n_sub, HEAD_DIM)
        return jnp.where(own_a[None, :, :, None, :, None], q8, 0).reshape(nb, 2 * n_sub // 2 * ROWS, D_MODEL)

    def qbd_b(q):
        q8 = tokens8(q).reshape(nb, 1, ROWS, n_sub, HEAD_DIM)
        return jnp.where(own_b[None, :, None, :, None], q8, 0).reshape(nb, n_sub * ROWS, D_MODEL)

    outs = {name: [] for name in ("akp", "avp", "bkp", "bvp", "bkip", "aks", "avs", "bks", "bvs", "bkis")}
    for i in range(depth):
        j = i // 2
        gn = row_vec(norm_mix[i])
        if i % 2 == 0:
            lam_init = 0.8 - 0.6 * math.exp(-0.3 * i)
            w = a_w_qkv[j].astype(BF16)
            qg, kg = _tile_gain(a_q_norm[j]), _tile_gain(a_k_norm[j])
            lam_p, subln = a_lambda[j].astype(F32), row_vec(a_subln[j])
            wo = a_w_o[j].astype(BF16)
            q, _, k16, v32, _, kt32, _, vt = _proj_a(xp, gn, w, qg, kg, cos_p, sin_p, tm_p, key_minor=True)
            op = _diff_prompt(q, k16, vt, lam_p, subln.reshape(LANES, 1), lam_init, batch,
                              min(1024, seq), min(512, seq))
            outs["akp"].append(kt32)
            outs["avp"].append(v32)
            q, k32, k16, v32, v16 = _proj_a(xs, gn, w, qg, kg, cos_s, sin_s, ms)
            vn = jnp.pad(v16.reshape(nb, n_new, D_MODEL), ((0, 0), (0, PAGE - n_new), (0, 0)))
            vn = vn.reshape(nb, PAGE * n_sub // 2, LANES)
            os_ = _diff_sample(qbd_a(q), ca_kt, cache_a_v, keys_t(k16), vn, pt, j, lam_p, subln, lam_init, n_new)
            os_ = os_[:, :n_new].reshape(ms, D_MODEL).astype(BF16)
            outs["aks"].append(k32)
            outs["avs"].append(v32)
        else:
            w = b_w_qkv[j].astype(BF16)
            wt = _pad_last(w[:, B_TAIL:], LANES)
            w = w[:, :B_TAIL]
            qg, kg, kig = _tile_gain(b_q_norm[j]), _tile_gain(b_k_norm[j]), _tile_gain(b_kidx_norm[j])
            wo = b_w_o[j].astype(BF16)
            assert tm_p == CK
            q, _, k16, _, _, qi, tail, ki16, kt32, vt32, vt, kit32 = _proj_b(
                xp, gn, w, wt, qg, kg, kig, cos_p, sin_p, tm_p, key_minor=True)
            wi_t = tail[:, HEAD_DIM:HEAD_DIM + N_HEADS_IDX].T
            op = _dsa_prompt(qi, wi_t, ki16, q, k16, vt, batch)
            outs["bkp"].append(kt32)
            outs["bvp"].append(vt32)
            outs["bkip"].append(kit32)
            q, k32, k16, v32, v16, qi, tail, ki16 = _proj_b(xs, gn, w, wt, qg, kg, kig, cos_s, sin_s, ms)
            qi8 = jnp.pad(qi.reshape(N_HEADS_IDX, nb, n_new, HEAD_DIM), ((0, 0), (0, 0), (0, ROWS - n_new), (0, 0)))
            qi8 = qi8.transpose(1, 0, 2, 3).reshape(nb, N_HEADS_IDX * ROWS, HEAD_DIM)
            wi8 = tokens8(tail[:, HEAD_DIM:HEAD_DIM + N_HEADS_IDX]).transpose(0, 2, 1)
            wi8 = jnp.broadcast_to(wi8.reshape(nb, N_HEADS_IDX * ROWS, 1), (nb, N_HEADS_IDX * ROWS, PAGE))
            keys, thr, cut = _dsa_index_sample(qi8, wi8, cb_kit, keys_t(ki16), pt, j, n_new)
            vnt = _pad_last(v16.reshape(nb, n_new, n_sub, HEAD_DIM).transpose(0, 2, 3, 1), PAGE)
            os_ = _dsa_sample(qbd_b(q), keys, thr, cut, cb_kt, cb_vt, keys_t(k16), vnt, pt, j, n_new)
            os_ = os_.reshape(nb, n_sub, ROWS, HEAD_DIM)[:, :, :n_new].transpose(0, 2, 1, 3)
            os_ = os_.reshape(ms, D_MODEL).astype(BF16)
            outs["bks"].append(k32)
            outs["bvs"].append(v32)
            outs["bkis"].append(tail[:, :HEAD_DIM])
        tail_w = (wo, row_vec(norm_ffn[i]), w_ffn_in[i].astype(BF16), w_ffn_out[i].astype(BF16),
                  row_vec(norm_ple[i]), w_ple_gate[i].astype(BF16), w_ple_proj[i].astype(BF16))
        xp = _tail(xp, op, p_prompt[i].reshape(mp, D_PLE), *tail_w, tm=tm_p)
        xs = _tail(xs, os_, p_sample[i].reshape(ms, D_PLE), *tail_w, tm=ms)

    n_a, n_b = D_MODEL // (2 * HEAD_DIM), D_MODEL // HEAD_DIM
    st = lambda name, shape: jnp.stack(outs[name]).reshape((len(outs[name]),) + shape)

    def st_t(name, heads):
        a = jnp.stack(outs[name]).reshape(len(outs[name]), batch, heads, HEAD_DIM, seq)
        return a.transpose(0, 1, 4, 2, 3)

    return (xp.reshape(batch, seq, D_MODEL), xs.reshape(nb, n_new, D_MODEL),
            st_t("akp", 2 * n_a), st("avp", (batch, seq, n_a, 2 * HEAD_DIM)),
            st_t("bkp", n_b), st_t("bvp", n_b),
            st_t("bkip", 1).reshape(len(outs["bkip"]), batch, seq, HEAD_DIM),
            st("aks", (nb, n_new, 2 * n_a, HEAD_DIM)), st("avs", (nb, n_new, n_a, 2 * HEAD_DIM)),
            st("bks", (nb, n_new, n_b, HEAD_DIM)), st("bvs", (nb, n_new, n_b, HEAD_DIM)),
            st("bkis", (nb, n_new, HEAD_DIM)))
```
